```python
import math
import jax
import jax.numpy as jnp
from jax import lax
import numpy as np

D_MODEL = 1024
BATCH = 2
SEQ = 16384
DEPTH = 2

CHUNK = 64
CONV_WIDTH = 4
EPS = 1e-6

RET_HEADS = 4
RET_DK = 128
RET_DV = 128
RET_QK_W = RET_HEADS * RET_DK
RET_V_W = RET_HEADS * RET_DV
ROPE_THETA = 10000.0

SSD_HEADS = 8
SSD_HEAD_DIM = 64
SSD_GROUPS = 2
SSD_STATE = 128
SSD_INNER = SSD_HEADS * SSD_HEAD_DIM
SSD_XBC_W = SSD_INNER + 2 * SSD_GROUPS * SSD_STATE

GDN_HEADS = 6
GDN_DK = 128
GDN_DV = 128
GDN_QK_W = GDN_HEADS * GDN_DK
GDN_V_W = GDN_HEADS * GDN_DV
GDN_QKV_W = 2 * GDN_QK_W + GDN_V_W

S5_CH = 256
S5_GROUP = 16
S5_GROUPS = S5_CH // S5_GROUP
S5_STATE = 64

D_FF = 4 * D_MODEL

MIX0_W = RET_V_W + SSD_INNER
MIX1_W = GDN_V_W + S5_CH
IN0_W = 2 * RET_QK_W + 2 * RET_V_W + SSD_INNER + SSD_XBC_W + SSD_HEADS
IN1_W = GDN_QKV_W + GDN_V_W + 2 * GDN_HEADS + S5_CH

kernel_name = "hybrid_retention_ssd_gdn_s5_trunk"


def rmsnorm(x, w):
    xf = x.astype(jnp.float32)
    y = xf * lax.rsqrt(jnp.mean(xf * xf, axis=-1, keepdims=True) + EPS)
    return (y * w.astype(jnp.float32)).astype(x.dtype)


def unit_rms(x):
    return x * lax.rsqrt(jnp.mean(x * x, axis=-1, keepdims=True) + EPS)


def l2norm(x):
    return x * lax.rsqrt(jnp.sum(x * x, axis=-1, keepdims=True) + EPS)


def causal_dwconv(x, w):
    k = w.shape[0]
    return lax.conv_general_dilated(
        x, w[:, None, :].astype(x.dtype), window_strides=(1,), padding=[(k - 1, 0)],
        dimension_numbers=("NWC", "WIO", "NWC"), feature_group_count=x.shape[-1])


def rotary(x, pos):
    half = x.shape[-1] // 2
    inv = ROPE_THETA ** (-jnp.arange(half, dtype=jnp.float32) / half)
    ang = pos[:, None] * inv[None, :]
    cos = jnp.cos(ang)[None, :, None, :]
    sin = jnp.sin(ang)[None, :, None, :]
    x1, x2 = x[..., :half], x[..., half:]
    return jnp.concatenate([x1 * cos - x2 * sin, x1 * sin + x2 * cos], axis=-1)


def retention_chunkwise(q, k, v):
    bsz, seqlen, nh, dk = q.shape
    dv = v.shape[-1]
    nc = seqlen // CHUNK
    log_gamma = jnp.log(1.0 - 2.0 ** (-5.0 - jnp.arange(nh, dtype=jnp.float32)))
    q = q.reshape(bsz, nc, CHUNK, nh, dk) * (dk ** -0.5)
    k = k.reshape(bsz, nc, CHUNK, nh, dk)
    v = v.reshape(bsz, nc, CHUNK, nh, dv)
    idx = jnp.arange(CHUNK, dtype=jnp.float32)
    diff = idx[:, None] - idx[None, :]
    causal = diff >= 0
    dmask = jnp.exp(jnp.where(causal[None], log_gamma[:, None, None] * diff[None], -jnp.inf))
    scores = jnp.einsum("bclhd,bcshd->bchls", q, k) * dmask[None, None]
    y_intra = jnp.einsum("bchls,bcshe->bclhe", scores, v)
    k_w = k * jnp.exp(log_gamma[None, :] * (CHUNK - 1.0 - idx)[:, None])[None, None, :, :, None]
    chunk_kv = jnp.einsum("bclhd,bclhe->bchde", k_w, v)
    chunk_decay = jnp.exp(log_gamma * CHUNK)[None, :, None, None]

    def step(state, kv):
        return state * chunk_decay + kv, state

    init = jnp.zeros((bsz, nh, dk, dv), jnp.float32)
    _, prev = lax.scan(step, init, jnp.moveaxis(chunk_kv, 1, 0))
    prev = jnp.moveaxis(prev, 0, 1)
    q_w = q * jnp.exp(log_gamma[None, :] * (idx + 1.0)[:, None])[None, None, :, :, None]
    y_inter = jnp.einsum("bclhd,bchde->bclhe", q_w, prev)
    return (y_intra + y_inter).reshape(bsz, seqlen, nh, dv)


def ssd_chunked(x, dt, a, bm, cm):
    bsz, seqlen, nh, p = x.shape
    ng, n = bm.shape[-2:]
    nj = nh // ng
    nc = seqlen // CHUNK
    xd = (x * dt[..., None]).reshape(bsz, nc, CHUNK, ng, nj, p)
    la = (dt * a).reshape(bsz, nc, CHUNK, ng, nj)
    la_cum = jnp.cumsum(la, axis=2)
    bc = bm.reshape(bsz, nc, CHUNK, ng, n)
    cc = cm.reshape(bsz, nc, CHUNK, ng, n)
    causal = jnp.tril(jnp.ones((CHUNK, CHUNK), bool))[None, None, :, :, None, None]
    seg = la_cum[:, :, :, None] - la_cum[:, :, None, :]
    lmat = jnp.exp(jnp.where(causal, seg, -jnp.inf))
    cb = jnp.einsum("bclgn,bcsgn->bclsg", cc, bc)
    y_diag = jnp.einsum("bclsgj,bcsgjp->bclgjp", cb[..., None] * lmat, xd)
    decay_to_end = jnp.exp(la_cum[:, :, -1:] - la_cum)
    chunk_states = jnp.einsum("bclgn,bclgj,bclgjp->bcgjpn", bc, decay_to_end, xd)
    chunk_decay = jnp.exp(la_cum[:, :, -1])

    def step(state, inp):
        s, d = inp
        return state * d[..., None, None] + s, state

    init = jnp.zeros((bsz, ng, nj, p, n), jnp.float32)
    _, prev = lax.scan(step, init, (jnp.moveaxis(chunk_states, 1, 0), jnp.moveaxis(chunk_decay, 1, 0)))
    prev = jnp.moveaxis(prev, 0, 1)
    y_off = jnp.einsum("bclgn,bcgjpn,bclgj->bclgjp", cc, prev, jnp.exp(la_cum))
    return (y_diag + y_off).reshape(bsz, seqlen, nh, p)


def gated_delta_chunked(q, k, v, g, beta):
    bsz, seqlen, nh, dk = q.shape
    dv = v.shape[-1]
    nc = seqlen // CHUNK

    def to_chunks(t):
        return t.reshape(bsz, nc, CHUNK, nh, t.shape[-1]).transpose(0, 3, 1, 2, 4)

    q = to_chunks(q * (dk ** -0.5))
    k = to_chunks(k)
    v = to_chunks(v)
    g = g.reshape(bsz, nc, CHUNK, nh).transpose(0, 3, 1, 2)
    beta = beta.reshape(bsz, nc, CHUNK, nh).transpose(0, 3, 1, 2)
    g_cum = jnp.cumsum(g, axis=-1)
    causal = jnp.tril(jnp.ones((CHUNK, CHUNK), bool))
    strict = jnp.tril(jnp.ones((CHUNK, CHUNK), bool), k=-1)
    decay = jnp.exp(jnp.where(causal, g_cum[..., :, None] - g_cum[..., None, :], -jnp.inf))
    k_beta = k * beta[..., None]
    v_beta = v * beta[..., None]
    lower = jnp.where(strict, jnp.einsum("bhcld,bhcsd->bhcls", k_beta, k) * decay, 0.0)
    eye = jnp.eye(CHUNK, dtype=jnp.float32)
    t_inv = lax.linalg.triangular_solve(eye + lower, jnp.broadcast_to(eye, lower.shape),
                                        left_side=True, lower=True)
    u = t_inv @ v_beta
    w = t_inv @ (k_beta * jnp.exp(g_cum)[..., None])
    attn = jnp.where(causal, jnp.einsum("bhcld,bhcsd->bhcls", q, k) * decay, 0.0)
    q_g = q * jnp.exp(g_cum)[..., None]
    k_tail = k * jnp.exp(g_cum[..., -1:] - g_cum)[..., None]
    chunk_decay = jnp.exp(g_cum[..., -1])

    def step(s, inp):
        q_i, w_i, u_i, a_i, kt_i, d_i = inp
        v_new = u_i - w_i @ s
        o = q_i @ s + a_i @ v_new
        s = s * d_i[..., None, None] + jnp.einsum("bhcd,bhce->bhde", kt_i, v_new)
        return s, o

    xs = tuple(jnp.moveaxis(t, 2, 0) for t in (q_g, w, u, attn, k_tail, chunk_decay))
    init = jnp.zeros((bsz, nh, dk, dv), jnp.float32)
    _, o = lax.scan(step, init, xs)
    return o.transpose(1, 0, 3, 2, 4).reshape(bsz, seqlen, nh, dv)


def s5_group_ssm(u, a_re, a_im, log_step, b_re, b_im, c_re, c_im, d_skip, w_glu, b_glu):
    f32 = jnp.float32
    bsz, seqlen, _ = u.shape
    ug = u.reshape(bsz, seqlen, S5_GROUPS, S5_GROUP)
    lam = lax.complex(a_re.astype(f32), a_im.astype(f32))
    step = jnp.exp(log_step.astype(f32))[:, None]
    lam_bar = jnp.exp(lam * step)
    b_mat = lax.complex(b_re.astype(f32), b_im.astype(f32))
    b_bar = ((lam_bar - 1.0) / lam)[..., None] * b_mat
    bu = jnp.einsum("blgc,gnc->blgn", ug.astype(jnp.complex64), b_bar)
    a_seq = jnp.broadcast_to(lam_bar, bu.shape)

    def combine(left, right):
        a_l, b_l = left
        a_r, b_r = right
        return a_r * a_l, a_r * b_l + b_r

    _, h = lax.associative_scan(combine, (a_seq, bu), axis=1)
    c_mat = lax.complex(c_re.astype(f32), c_im.astype(f32))
    y = jnp.einsum("blgn,gcn->blgc", h, c_mat).real + d_skip.astype(f32).reshape(S5_GROUPS, S5_GROUP) * ug
    y = jax.nn.gelu(y.reshape(bsz, seqlen, S5_CH))
    return y * jax.nn.sigmoid(y @ w_glu.astype(f32) + b_glu.astype(f32))


def retention_ssd_mixer(h, w_in, ssd_conv_w, ssd_conv_b, ssd_dt_bias, ssd_A_log, ssd_D, ssd_norm_w, w_out):
    f32 = jnp.float32
    bsz, seqlen, _ = h.shape
    proj = h @ w_in
    offs = np.cumsum([RET_QK_W, RET_QK_W, RET_V_W, RET_V_W, SSD_INNER, SSD_XBC_W]).tolist()
    q, k, v, gate, z, xbc, dt = jnp.split(proj, offs, axis=-1)
    pos = jnp.arange(seqlen, dtype=f32)
    q = rotary(q.astype(f32).reshape(bsz, seqlen, RET_HEADS, RET_DK), pos)
    k = rotary(k.astype(f32).reshape(bsz, seqlen, RET_HEADS, RET_DK), pos)
    v = v.astype(f32).reshape(bsz, seqlen, RET_HEADS, RET_DV)
    r = unit_rms(retention_chunkwise(q, k, v)).reshape(bsz, seqlen, RET_V_W)
    ret_out = jax.nn.silu(gate.astype(f32)) * r
    xbc = jax.nn.silu(causal_dwconv(xbc, ssd_conv_w) + ssd_conv_b.astype(xbc.dtype)).astype(f32)
    xs, bm, cm = jnp.split(xbc, [SSD_INNER, SSD_INNER + SSD_GROUPS * SSD_STATE], axis=-1)
    dt = jax.nn.softplus(dt.astype(f32) + ssd_dt_bias.astype(f32))
    a = -jnp.exp(ssd_A_log.astype(f32))
    xs = xs.reshape(bsz, seqlen, SSD_HEADS, SSD_HEAD_DIM)
    y = ssd_chunked(xs, dt, a, bm.reshape(bsz, seqlen, SSD_GROUPS, SSD_STATE),
                    cm.reshape(bsz, seqlen, SSD_GROUPS, SSD_STATE))
    y = (y + ssd_D.astype(f32)[:, None] * xs).reshape(bsz, seqlen, SSD_INNER)
    yg = (y * jax.nn.silu(z.astype(f32))).reshape(bsz, seqlen, SSD_GROUPS, SSD_INNER // SSD_GROUPS)
    ssd_out = unit_rms(yg).reshape(bsz, seqlen, SSD_INNER) * ssd_norm_w.astype(f32)
    mixed = jnp.concatenate([ret_out, ssd_out], axis=-1).astype(h.dtype)
    return mixed @ w_out


def deltanet_s5_mixer(h, w_in, gdn_conv_w, gdn_A_log, gdn_dt_bias, gdn_norm_w,
                      s5_A_re, s5_A_im, s5_log_step, s5_B_re, s5_B_im, s5_C_re, s5_C_im,
                      s5_D, s5_w_glu, s5_b_glu, w_out):
    f32 = jnp.float32
    bsz, seqlen, _ = h.shape
    proj = h @ w_in
    offs = np.cumsum([GDN_QKV_W, GDN_V_W, GDN_HEADS, GDN_HEADS]).tolist()
    qkv, z, b_raw, a_raw, u = jnp.split(proj, offs, axis=-1)
    qkv = jax.nn.silu(causal_dwconv(qkv, gdn_conv_w)).astype(f32)
    q, k, v = jnp.split(qkv, [GDN_QK_W, 2 * GDN_QK_W], axis=-1)
    q = l2norm(q.reshape(bsz, seqlen, GDN_HEADS, GDN_DK))
    k = l2norm(k.reshape(bsz, seqlen, GDN_HEADS, GDN_DK))
    v = v.reshape(bsz, seqlen, GDN_HEADS, GDN_DV)
    beta = jax.nn.sigmoid(b_raw.astype(f32))
    g = -jnp.exp(gdn_A_log.astype(f32)) * jax.nn.softplus(a_raw.astype(f32) + gdn_dt_bias.astype(f32))
    o = gated_delta_chunked(q, k, v, g, beta)
    o = unit_rms(o) * gdn_norm_w.astype(f32) * jax.nn.silu(z.astype(f32).reshape(bsz, seqlen, GDN_HEADS, GDN_DV))
    gdn_out = o.reshape(bsz, seqlen, GDN_V_W)
    s5_out = s5_group_ssm(u.astype(f32), s5_A_re, s5_A_im, s5_log_step, s5_B_re, s5_B_im,
                          s5_C_re, s5_C_im, s5_D, s5_w_glu, s5_b_glu)
    mixed = jnp.concatenate([gdn_out, s5_out], axis=-1).astype(h.dtype)
    return mixed @ w_out


def sqrelu_mlp(h, w_up, w_down):
    a = jax.nn.relu(h @ w_up)
    return (a * a) @ w_down


def setup_inputs(seed: int = 0) -> dict:
    key = jax.random.key(seed)
    ks = jax.random.split(key, 34)
    f32 = jnp.float32

    def nrm(i, shape, scale):
        return scale * jax.random.normal(ks[i], shape, f32)

    def gain(i, n):
        return 1.0 + nrm(i, (n,), 0.02)

    def dt_bias(i, n):
        dt = jnp.exp(jax.random.uniform(ks[i], (n,), f32, math.log(1e-3), math.log(1e-1)))
        return dt + jnp.log(-jnp.expm1(-dt))

    def a_log(i, n):
        return jnp.log(jax.random.uniform(ks[i], (n,), f32, 1.0, 16.0))

    dm = D_MODEL ** -0.5
    return {
        "x": nrm(0, (BATCH, SEQ, D_MODEL), 1.0),
        "l0_norm_mix": gain(1, D_MODEL),
        "l0_w_in": nrm(2, (D_MODEL, IN0_W), dm),
        "ssd_conv_w": nrm(3, (CONV_WIDTH, SSD_XBC_W), CONV_WIDTH ** -0.5),
        "ssd_conv_b": nrm(4, (SSD_XBC_W,), 0.02),
        "ssd_dt_bias": dt_bias(5, SSD_HEADS),
        "ssd_A_log": a_log(6, SSD_HEADS),
        "ssd_D": gain(7, SSD_HEADS),
        "ssd_norm_w": gain(8, SSD_INNER),
        "l0_w_out": nrm(9, (MIX0_W, D_MODEL), MIX0_W ** -0.5),
        "l0_norm_mlp": gain(10, D_MODEL),
        "l0_w_up": nrm(11, (D_MODEL, D_FF), dm),
        "l0_w_down": nrm(12, (D_FF, D_MODEL), D_FF ** -0.5),
        "l1_norm_mix": gain(13, D_MODEL),
        "l1_w_in": nrm(14, (D_MODEL, IN1_W), dm),
        "gdn_conv_w": nrm(15, (CONV_WIDTH, GDN_QKV_W), CONV_WIDTH ** -0.5),
        "gdn_A_log": a_log(16, GDN_HEADS),
        "gdn_dt_bias": dt_bias(17, GDN_HEADS),
        "gdn_norm_w": gain(18, GDN_DV),
        "s5_A_re": -0.5 + nrm(19, (S5_GROUPS, S5_STATE), 0.01),
        "s5_A_im": math.pi * jnp.arange(S5_STATE, dtype=f32)[None, :] + nrm(20, (S5_GROUPS, S5_STATE), 0.01),
        "s5_log_step": jax.random.uniform(ks[21], (S5_GROUPS,), f32, math.log(1e-3), math.log(1e-1)),
        "s5_B_re": nrm(22, (S5_GROUPS, S5_STATE, S5_GROUP), (2 * S5_GROUP) ** -0.5),
        "s5_B_im": nrm(23, (S5_GROUPS, S5_STATE, S5_GROUP), (2 * S5_GROUP) ** -0.5),
        "s5_C_re": nrm(24, (S5_GROUPS, S5_GROUP, S5_STATE), (2 * S5_STATE) ** -0.5),
        "s5_C_im": nrm(25, (S5_GROUPS, S5_GROUP, S5_STATE), (2 * S5_STATE) ** -0.5),
        "s5_D": nrm(26, (S5_CH,), 1.0),
        "s5_w_glu": nrm(27, (S5_CH, S5_CH), S5_CH ** -0.5),
        "s5_b_glu": nrm(28, (S5_CH,), 0.02),
        "l1_w_out": nrm(29, (MIX1_W, D_MODEL), MIX1_W ** -0.5),
        "l1_norm_mlp": gain(30, D_MODEL),
        "l1_w_up": nrm(31, (D_MODEL, D_FF), dm),
        "l1_w_down": nrm(32, (D_FF, D_MODEL), D_FF ** -0.5),
        "final_norm": gain(33, D_MODEL),
    }


def reference(x, l0_norm_mix, l0_w_in, ssd_conv_w, ssd_conv_b, ssd_dt_bias, ssd_A_log, ssd_D,
              ssd_norm_w, l0_w_out, l0_norm_mlp, l0_w_up, l0_w_down, l1_norm_mix, l1_w_in,
              gdn_conv_w, gdn_A_log, gdn_dt_bias, gdn_norm_w, s5_A_re, s5_A_im, s5_log_step,
              s5_B_re, s5_B_im, s5_C_re, s5_C_im, s5_D, s5_w_glu, s5_b_glu, l1_w_out,
              l1_norm_mlp, l1_w_up, l1_w_down, final_norm):
    for layer in range(DEPTH):
        if layer % 2 == 0:
            x = x + retention_ssd_mixer(rmsnorm(x, l0_norm_mix), l0_w_in, ssd_conv_w, ssd_conv_b,
                                        ssd_dt_bias, ssd_A_log, ssd_D, ssd_norm_w, l0_w_out)
            x = x + sqrelu_mlp(rmsnorm(x, l0_norm_mlp), l0_w_up, l0_w_down)
        else:
            x = x + deltanet_s5_mixer(rmsnorm(x, l1_norm_mix), l1_w_in, gdn_conv_w, gdn_A_log,
                                      gdn_dt_bias, gdn_norm_w, s5_A_re, s5_A_im, s5_log_step,
                                      s5_B_re, s5_B_im, s5_C_re, s5_C_im, s5_D, s5_w_glu,
                                      s5_b_glu, l1_w_out)
            x = x + sqrelu_mlp(rmsnorm(x, l1_norm_mlp), l1_w_up, l1_w_down)
    return rmsnorm(x, final_norm)
```

```python
import functools
import math

import jax
import jax.numpy as jnp
from jax import lax
from jax.experimental import pallas as pl
from jax.experimental.pallas import tpu as pltpu

F32 = jnp.float32
BF16 = jnp.bfloat16
EPS = 1e-6
CONV_WIDTH = 4
HALO = 8

RET_HEADS, RET_DK = 4, 128
SSD_HEADS, SSD_P, SSD_GROUPS, SSD_N = 8, 64, 2, 128
GDN_HEADS, GDN_DK = 6, 128
GDN_CHUNK = 64
S5_CH, S5_GROUPS, S5_GROUP, S5_STATE = 256, 16, 16, 64
ROPE_THETA = 10000.0
LANES = 128

VMEM_LIMIT = 56 * 1024 * 1024


def _full(shape):
    nd = len(shape)
    return pl.BlockSpec(shape, lambda *_: (0,) * nd)


def _params(n_axes):
    return pltpu.CompilerParams(dimension_semantics=("arbitrary",) * n_axes, vmem_limit_bytes=VMEM_LIMIT)


def _mm(a, b):
    return jnp.dot(a.astype(BF16), b.astype(BF16), preferred_element_type=F32)


def _mm_nt(a, b):
    return lax.dot_general(a.astype(BF16), b.astype(BF16), (((1,), (1,)), ((), ())), preferred_element_type=F32)


def _mm_hi(a, b):
    return jnp.dot(a, b, preferred_element_type=F32, precision=lax.Precision.HIGHEST)


def _sigmoid(x):
    return 1.0 / (1.0 + jnp.exp(-x))


def _silu(x):
    return x * _sigmoid(x)


def _softplus(x):
    return jnp.maximum(x, 0.0) + jnp.log(1.0 + jnp.exp(-jnp.abs(x)))


def _rms(x):
    return x * lax.rsqrt(jnp.mean(x * x, axis=-1, keepdims=True) + EPS)


def _norm_proj_body(x_ref, g_ref, *refs, n_out):
    w_refs, o_refs = refs[:n_out], refs[n_out:]
    xb = (_rms(x_ref[...]) * g_ref[...]).astype(BF16)
    for w_ref, o_ref in zip(w_refs, o_refs):
        o_ref[...] = jnp.dot(xb, w_ref[...], preferred_element_type=F32)


def _norm_proj(x2d, gain, weights, tm=256):
    t, d = x2d.shape
    n_out = len(weights)
    return pl.pallas_call(
        functools.partial(_norm_proj_body, n_out=n_out),
        grid=(t // tm,),
        in_specs=[pl.BlockSpec((tm, d), lambda i: (i, 0)), _full((1, d))] + [_full(w.shape) for w in weights],
        out_specs=[pl.BlockSpec((tm, w.shape[1]), lambda i: (i, 0)) for w in weights],
        out_shape=[jax.ShapeDtypeStruct((t, w.shape[1]), F32) for w in weights],
        compiler_params=_params(1),
        name="norm_proj",
    )(x2d, gain, *weights)


def _out_mlp_body(x_ref, *refs, n_mix, ff_chunk, final):
    mix_refs, wo_refs = refs[:n_mix], refs[n_mix:2 * n_mix]
    rest = refs[2 * n_mix:]
    if final:
        g_ref, wu_ref, wd_ref, gf_ref, o_ref = rest
    else:
        g_ref, wu_ref, wd_ref, o_ref = rest
    x1 = x_ref[...]
    for m_ref, w_ref in zip(mix_refs, wo_refs):
        x1 = x1 + jnp.dot(m_ref[...].astype(BF16), w_ref[...], preferred_element_type=F32)
    xb = (_rms(x1) * g_ref[...]).astype(BF16)
    d_ff = wu_ref.shape[1]
    mlp = None
    for c in range(d_ff // ff_chunk):
        a = jnp.dot(xb, wu_ref[:, c * ff_chunk:(c + 1) * ff_chunk], preferred_element_type=F32)
        a = jnp.maximum(a, 0.0)
        t = jnp.dot((a * a).astype(BF16), wd_ref[c * ff_chunk:(c + 1) * ff_chunk, :],
                    preferred_element_type=F32)
        mlp = t if mlp is None else mlp + t
    acc = x1 + mlp
    if final:
        acc = _rms(acc) * gf_ref[...]
    o_ref[...] = acc


def _out_mlp(x2d, mixes, w_outs, gain, w_up, w_down, final_gain=None, tm=256, ff_chunk=1024):
    t, d = x2d.shape
    n_mix = len(mixes)
    final = final_gain is not None
    row = lambda w: pl.BlockSpec((tm, w), lambda i: (i, 0))
    in_specs = [row(d)] + [row(m.shape[1]) for m in mixes] + [_full(w.shape) for w in w_outs]
    in_specs += [_full((1, d)), _full(w_up.shape), _full(w_down.shape)]
    args = [x2d, *mixes, *w_outs, gain, w_up, w_down]
    if final:
        in_specs.append(_full((1, d)))
        args.append(final_gain)
    return pl.pallas_call(
        functools.partial(_out_mlp_body, n_mix=n_mix, ff_chunk=ff_chunk, final=final),
        grid=(t // tm,),
        in_specs=in_specs,
        out_specs=row(d),
        out_shape=jax.ShapeDtypeStruct((t, d), F32),
        compiler_params=_params(1),
        name="out_mlp",
    )(*args)


def _retention_body(q_ref, k_ref, v_ref, g_ref, inv_ref, sgn_ref, o_ref, s_ref, dm_ref, *, chunk):
    c = chunk
    i = pl.program_id(1)
    gammas = [math.log(1.0 - 2.0 ** (-5.0 - h)) for h in range(RET_HEADS)]

    @pl.when(i == 0)
    def _():
        s_ref[...] = jnp.zeros_like(s_ref)
        rowi = lax.broadcasted_iota(jnp.int32, (c, c), 0)
        coli = lax.broadcasted_iota(jnp.int32, (c, c), 1)
        diff = (rowi - coli).astype(F32)
        for h in range(RET_HEADS):
            dm_ref[h] = jnp.where(rowi >= coli, jnp.exp(gammas[h] * diff), 0.0)

    idx = lax.broadcasted_iota(jnp.int32, (c, 1), 0).astype(F32)
    pos = idx + (i * c).astype(F32)
    ang = pos * inv_ref[...]
    cos = jnp.cos(ang)
    sin = jnp.sin(ang) * sgn_ref[...]

    def rot(x):
        return x * cos + pltpu.roll(x, RET_DK // 2, 1) * sin

    for h in range(RET_HEADS):
        sl = slice(h * RET_DK, (h + 1) * RET_DK)
        lg = gammas[h]
        qh = rot(q_ref[:, sl]) * (RET_DK ** -0.5)
        kh = rot(k_ref[:, sl])
        vh = v_ref[:, sl]
        scores = _mm_nt(qh, kh) * dm_ref[h]
        state = s_ref[h]
        y = _mm(scores, vh) + _mm(qh * jnp.exp(lg * (idx + 1.0)), state)
        kw = kh * jnp.exp(lg * (c - 1.0 - idx))
        s_ref[h] = state * math.exp(lg * c) + _mm(kw.T, vh)
        gt = g_ref[:, sl]
        o_ref[:, sl] = _silu(gt) * _rms(y)


def _retention(q, k, v, gate, inv2, sgn, bsz, seqlen, chunk=256):
    nt = seqlen // chunk
    w = RET_HEADS * RET_DK
    row = pl.BlockSpec((chunk, w), lambda b, i: (b * nt + i, 0))
    return pl.pallas_call(
        functools.partial(_retention_body, chunk=chunk),
        grid=(bsz, nt),
        in_specs=[row, row, row, row, _full((1, RET_DK)), _full((1, RET_DK))],
        out_specs=row,
        out_shape=jax.ShapeDtypeStruct((bsz * seqlen, w), F32),
        scratch_shapes=[pltpu.VMEM((RET_HEADS, RET_DK, RET_DK), F32),
                        pltpu.VMEM((RET_HEADS, chunk, chunk), F32)],
        compiler_params=_params(2),
        name="retention",
    )(q, k, v, gate, inv2, sgn)


def _causal_conv(xe_ref, x_ref, w_ref, rows, first):
    @pl.when(first)
    def _():
        xe_ref[0:HALO, :] = jnp.zeros((HALO, xe_ref.shape[1]), F32)

    xe_ref[HALO:HALO + rows, :] = x_ref[...]
    acc = None
    for j in range(CONV_WIDTH):
        off = HALO - (CONV_WIDTH - 1) + j
        term = w_ref[j:j + 1, :] * xe_ref[off:off + rows, :]
        acc = term if acc is None else acc + term
    xe_ref[0:HALO, :] = xe_ref[rows:rows + HALO, :]
    return acc


def _ssd_body(z_ref, x_ref, dt_ref, cw_ref, cb_ref, dtb_ref, al_ref, de_ref, nw_ref, e8_ref,
              o_ref, xe_ref, s_ref, *, chunk):
    c = chunk
    i = pl.program_id(1)
    inner = SSD_HEADS * SSD_P
    gw = inner // SSD_GROUPS
    hpg = SSD_HEADS // SSD_GROUPS

    @pl.when(i == 0)
    def _():
        s_ref[...] = jnp.zeros_like(s_ref)

    xc = _silu(_causal_conv(xe_ref, x_ref, cw_ref, c, i == 0) + cb_ref[...])
    xs = xc[:, :inner]
    bm = xc[:, inner:inner + SSD_GROUPS * SSD_N]
    cm = xc[:, inner + SSD_GROUPS * SSD_N:]

    dtv = _softplus(dt_ref[...] + dtb_ref[...])
    la = dtv * (-jnp.exp(al_ref[...]))
    rowi = lax.broadcasted_iota(jnp.int32, (c, c), 0)
    coli = lax.broadcasted_iota(jnp.int32, (c, c), 1)
    causal = rowi >= coli
    cum = _mm_hi(causal.astype(F32), la)
    cum_t = cum.T
    e8 = e8_ref[...]
    dt_e = _mm_hi(dtv, e8)
    cum_e = _mm_hi(cum, e8)
    last_e = cum_e[c - 1:c, :]
    xd = xs * dt_e
    xdw = xd * jnp.exp(last_e - cum_e)
    lane = lax.broadcasted_iota(jnp.int32, (1, gw), 1)

    ys = []
    for g in range(SSD_GROUPS):
        bg = bm[:, g * SSD_N:(g + 1) * SSD_N]
        cg = cm[:, g * SSD_N:(g + 1) * SSD_N]
        gsl = slice(g * gw, (g + 1) * gw)
        cb = _mm_nt(cg, bg)
        xd_g = xd[:, gsl]
        yd = None
        for j in range(hpg):
            h = g * hpg + j
            seg = cum[:, h:h + 1] - cum_t[h:h + 1, :]
            lmat = jnp.exp(jnp.where(causal, seg, -jnp.inf))
            xm = jnp.where((lane >= SSD_P * j) & (lane < SSD_P * (j + 1)), xd_g, 0.0)
            t = _mm(cb * lmat, xm)
            yd = t if yd is None else yd + t
        state = s_ref[g]
        y_off = _mm(cg, state) * jnp.exp(cum_e[:, gsl])
        s_ref[g] = state * jnp.exp(last_e[:, gsl]) + _mm(bg.T, xdw[:, gsl])
        ys.append(yd + y_off)

    zz = z_ref[...]
    for g in range(SSD_GROUPS):
        gsl = slice(g * gw, (g + 1) * gw)
        yg = (ys[g] + de_ref[:, gsl] * xs[:, gsl]) * _silu(zz[:, gsl])
        o_ref[:, gsl] = _rms(yg) * nw_ref[:, gsl]


def _ssd(z, xbc, dt, conv_w, conv_b, dt_bias, a_log, d_e, norm_w, e8, bsz, seqlen, chunk=256):
    nt = seqlen // chunk
    inner = SSD_HEADS * SSD_P
    wx = xbc.shape[1]
    row = lambda w: pl.BlockSpec((chunk, w), lambda b, i: (b * nt + i, 0))
    return pl.pallas_call(
        functools.partial(_ssd_body, chunk=chunk),
        grid=(bsz, nt),
        in_specs=[row(inner), row(wx), row(LANES), _full(conv_w.shape), _full((1, wx)), _full((1, LANES)),
                  _full((1, LANES)), _full((1, inner)), _full((1, inner)), _full(e8.shape)],
        out_specs=row(inner),
        out_shape=jax.ShapeDtypeStruct((bsz * seqlen, inner), F32),
        scratch_shapes=[pltpu.VMEM((chunk + 2 * HALO, wx), F32),
                        pltpu.VMEM((SSD_GROUPS, SSD_N, inner // SSD_GROUPS), F32)],
        compiler_params=_params(2),
        name="ssd",
    )(z, xbc, dt, conv_w, conv_b, dt_bias, a_log, d_e, norm_w, e8)


def _gdn_body(qkv_ref, z_ref, b_ref, a_ref, cw_ref, al_ref, dtb_ref, nw_ref, e6_ref,
              o_ref, xe_ref, s_ref, vn_ref, *, tile):
    tt = tile
    ck = GDN_CHUNK
    nck = tt // ck
    i = pl.program_id(1)
    qk_w = GDN_HEADS * GDN_DK

    @pl.when(i == 0)
    def _():
        s_ref[...] = jnp.zeros_like(s_ref)

    vn_ref[...] = jnp.zeros_like(vn_ref)

    xc = _silu(_causal_conv(xe_ref, qkv_ref, cw_ref, tt, i == 0))
    beta = _sigmoid(b_ref[...])
    gl = -jnp.exp(al_ref[...]) * _softplus(a_ref[...] + dtb_ref[...])

    rowi = lax.broadcasted_iota(jnp.int32, (tt, tt), 0)
    coli = lax.broadcasted_iota(jnp.int32, (tt, tt), 1)

    def same_block(bits):
        return (rowi >> bits) == (coli >> bits)

    blk = same_block(6)
    causal = blk & (rowi >= coli)
    strict = blk & (rowi > coli)
    gc = _mm_hi(causal.astype(F32), gl)
    gc_t = gc.T
    e6 = e6_ref[...]
    gc_e = _mm_hi(gc, e6)
    beta_e = _mm_hi(beta, e6)
    eg_e = jnp.exp(gc_e)
    glast_rows = [jnp.broadcast_to(gc_e[(cix + 1) * ck - 1:(cix + 1) * ck, :], (ck, qk_w)) for cix in range(nck)]
    glast_e = jnp.concatenate(glast_rows, axis=0)
    tail_e = jnp.exp(glast_e - gc_e)
    dlast_e = jnp.exp(glast_e)

    eye = (rowi == coli).astype(F32)
    base_mask = same_block(3) & (rowi > coli)
    level_masks = [same_block(b + 1) & (~same_block(b)) & (rowi > coli) for b in (3, 4, 5)]

    for h in range(GDN_HEADS):
        sl = slice(h * GDN_DK, (h + 1) * GDN_DK)
        qh = xc[:, sl]
        kh = xc[:, qk_w + h * GDN_DK: qk_w + (h + 1) * GDN_DK]
        vh = xc[:, 2 * qk_w + h * GDN_DK: 2 * qk_w + (h + 1) * GDN_DK]
        qh = qh * lax.rsqrt(jnp.sum(qh * qh, axis=-1, keepdims=True) + EPS) * (GDN_DK ** -0.5)
        kh = kh * lax.rsqrt(jnp.sum(kh * kh, axis=-1, keepdims=True) + EPS)
        bh = beta_e[:, sl]
        kb = kh * bh
        vb = vh * bh
        seg = gc[:, h:h + 1] - gc_t[h:h + 1, :]
        decay = jnp.exp(jnp.where(causal, seg, -jnp.inf))
        lower = jnp.where(strict, _mm_nt(kb, kh) * decay, 0.0)
        attn = _mm_nt(qh, kh) * decay

        d8 = jnp.where(base_mask, lower, 0.0)
        x2 = _mm(d8, d8)
        x4 = _mm(x2, x2)
        tinv = eye - d8
        tinv = tinv + _mm(tinv, x2)
        tinv = tinv + _mm(tinv, x4)
        for lm in level_masks:
            e = jnp.where(lm, lower, 0.0)
            tinv = tinv - _mm(tinv, _mm(e, tinv))

        u = _mm(tinv, vb)
        w = _mm(tinv, kb * eg_e[:, sl])
        qg = qh * eg_e[:, sl]
        kt = kh * tail_e[:, sl]

        state = s_ref[h]
        outs = []
        for cix in range(nck):
            rs = slice(cix * ck, (cix + 1) * ck)
            v_new = u[rs] - _mm(w[rs], state)
            vn_ref[h, rs, :] = v_new
            outs.append(_mm(qg[rs], state) + _mm(attn[rs], vn_ref[h]))
            state = state * dlast_e[cix * ck:cix * ck + 1, sl] + _mm(kt[rs].T, v_new)
        s_ref[h] = state
        o = jnp.concatenate(outs, axis=0)
        o_ref[:, sl] = _rms(o) * nw_ref[...] * _silu(z_ref[:, sl])


def _gdn(qkv, z, b_raw, a_raw, conv_w, a_log, dt_bias, norm_w, e6, bsz, seqlen, tile=256):
    nt = seqlen // tile
    wq = qkv.shape[1]
    wv = GDN_HEADS * GDN_DK
    row = lambda w: pl.BlockSpec((tile, w), lambda b, i: (b * nt + i, 0))
    return pl.pallas_call(
        functools.partial(_gdn_body, tile=tile),
        grid=(bsz, nt),
        in_specs=[row(wq), row(wv), row(LANES), row(LANES), _full(conv_w.shape), _full((1, LANES)),
                  _full((1, LANES)), _full((1, GDN_DK)), _full(e6.shape)],
        out_specs=row(wv),
        out_shape=jax.ShapeDtypeStruct((bsz * seqlen, wv), F32),
        scratch_shapes=[pltpu.VMEM((tile + 2 * HALO, wq), F32),
                        pltpu.VMEM((GDN_HEADS, GDN_DK, GDN_DK), F32),
                        pltpu.VMEM((GDN_HEADS, tile, GDN_DK), F32)],
        compiler_params=_params(2),
        name="gdn",
    )(qkv, z, b_raw, a_raw, conv_w, a_log, dt_bias, norm_w, e6)


def _s5_body(u_ref, are_ref, aim_ref, ls_ref, bre_ref, bim_ref, cre_ref, cim_ref, d_ref, wg_ref, bg_ref,
             o_ref, pre_ref, pim_ref, h_ref, hre_ref, him_ref, *, tile):
    tt = tile
    ns = S5_GROUPS * S5_STATE
    b = pl.program_id(0)
    i = pl.program_id(1)

    @pl.when((b == 0) & (i == 0))
    def _():
        tpos = lax.broadcasted_iota(jnp.int32, (tt, 1), 0).astype(F32) + 1.0
        step = jnp.exp(ls_ref[...])
        mag = jnp.exp(tpos * (are_ref[...] * step))
        ang = tpos * (aim_ref[...] * step)
        pre_ref[...] = mag * jnp.cos(ang)
        pim_ref[...] = mag * jnp.sin(ang)

    @pl.when(i == 0)
    def _():
        h_ref[...] = jnp.zeros_like(h_ref)

    u = u_ref[...]
    bu_re = _mm(u, bre_ref[...])
    bu_im = _mm(u, bim_ref[...])
    a_re, a_im = are_ref[...], aim_ref[...]
    n_re, n_im = pre_ref[0:1, :] - 1.0, pim_ref[0:1, :]
    den = a_re * a_re + a_im * a_im
    c_re = (n_re * a_re + n_im * a_im) / den
    c_im = (n_im * a_re - n_re * a_im) / den
    x_re = c_re * bu_re - c_im * bu_im
    x_im = c_re * bu_im + c_im * bu_re

    rows = lax.broadcasted_iota(jnp.int32, (tt, LANES), 0)
    nsteps = tt.bit_length() - 1

    def shift(x, s):
        if s % 8 == 0:
            return jnp.concatenate([jnp.zeros((s, LANES), F32), x[:tt - s]], axis=0)
        return jnp.where(rows >= s, pltpu.roll(x, s, 0), 0.0)

    for j in range(ns // LANES):
        ls = slice(j * LANES, (j + 1) * LANES)
        hr, hi = x_re[:, ls], x_im[:, ls]
        for kk in range(nsteps):
            s = 1 << kk
            ar, ai = pre_ref[s - 1:s, ls], pim_ref[s - 1:s, ls]
            sr, si = shift(hr, s), shift(hi, s)
            hr, hi = hr + ar * sr - ai * si, hi + ar * si + ai * sr
        cr, ci = h_ref[0:1, ls], h_ref[1:2, ls]
        pr, pi = pre_ref[:, ls], pim_ref[:, ls]
        hr, hi = hr + pr * cr - pi * ci, hi + pr * ci + pi * cr
        h_ref[0:1, ls] = hr[tt - 1:tt, :]
        h_ref[1:2, ls] = hi[tt - 1:tt, :]
        hre_ref[:, ls] = hr
        him_ref[:, ls] = hi

    y = _mm(hre_ref[...], cre_ref[...]) - _mm(him_ref[...], cim_ref[...]) + d_ref[...] * u
    y = jax.nn.gelu(y)
    o_ref[...] = y * _sigmoid(_mm(y, wg_ref[...]) + bg_ref[...])


def _s5(u, a_re, a_im, log_step, b_re, b_im, c_re, c_im, d_skip, w_glu, b_glu, bsz, seqlen, tile=128):
    nt = seqlen // tile
    ns = S5_GROUPS * S5_STATE
    row = pl.BlockSpec((tile, S5_CH), lambda b, i: (b * nt + i, 0))
    vec = _full((1, ns))
    return pl.pallas_call(
        functools.partial(_s5_body, tile=tile),
        grid=(bsz, nt),
        in_specs=[row, vec, vec, vec, _full((S5_CH, ns)), _full((S5_CH, ns)), _full((ns, S5_CH)),
                  _full((ns, S5_CH)), _full((1, S5_CH)), _full((S5_CH, S5_CH)), _full((1, S5_CH))],
        out_specs=row,
        out_shape=jax.ShapeDtypeStruct((bsz * seqlen, S5_CH), F32),
        scratch_shapes=[pltpu.VMEM((tile, ns), F32), pltpu.VMEM((tile, ns), F32), pltpu.VMEM((8, ns), F32),
                        pltpu.VMEM((tile, ns), F32), pltpu.VMEM((tile, ns), F32)],
        compiler_params=_params(2),
        name="s5",
    )(u, a_re, a_im, log_step, b_re, b_im, c_re, c_im, d_skip, w_glu, b_glu)


def _pad_lanes(v, width=LANES):
    v = v.reshape(1, -1).astype(F32)
    return jnp.pad(v, ((0, 0), (0, width - v.shape[1])))


def _pad_cols(w, width=LANES):
    return jnp.pad(w, ((0, 0), (0, width - w.shape[1])))


def _head_expander(n_heads, head_width, rows=LANES):
    r = jnp.arange(rows)[:, None]
    col_head = jnp.arange(n_heads * head_width)[None, :] // head_width
    return (r == col_head).astype(F32)


def _block_diag(blocks):
    g, r, c = blocks.shape
    eye = jnp.eye(g, dtype=blocks.dtype)
    return (eye[:, None, :, None] * blocks[:, :, None, :]).reshape(g * r, g * c)


def kernel(x, l0_norm_mix, l0_w_in, ssd_conv_w, ssd_conv_b, ssd_dt_bias, ssd_A_log, ssd_D, ssd_norm_w, l0_w_out, l0_norm_mlp, l0_w_up, l0_w_down, l1_norm_mix, l1_w_in, gdn_conv_w, gdn_A_log, gdn_dt_bias, gdn_norm_w, s5_A_re, s5_A_im, s5_log_step, s5_B_re, s5_B_im, s5_C_re, s5_C_im, s5_D, s5_w_glu, s5_b_glu, l1_w_out, l1_norm_mlp, l1_w_up, l1_w_down, final_norm):
    bsz, seqlen, d = x.shape
    x2d = x.reshape(bsz * seqlen, d)
    row = lambda v: v.reshape(1, -1).astype(F32)
    bf = lambda w: w.astype(BF16)

    ret_w = RET_HEADS * RET_DK
    ssd_inner = SSD_HEADS * SSD_P
    ssd_xbc = ssd_inner + 2 * SSD_GROUPS * SSD_N
    offs, acc = [], 0
    for wdt in (ret_w, ret_w, ret_w, ret_w, ssd_inner, ssd_xbc, SSD_HEADS):
        offs.append((acc, acc + wdt))
        acc += wdt
    w0 = [l0_w_in[:, a:b] for a, b in offs]
    w0[-1] = _pad_cols(w0[-1])
    q, k, v, gate, z, xbc, dt = _norm_proj(x2d, row(l0_norm_mix), [bf(w) for w in w0])

    half = RET_DK // 2
    inv = ROPE_THETA ** (-jnp.arange(half, dtype=F32) / half)
    inv2 = jnp.concatenate([inv, inv]).reshape(1, RET_DK)
    sgn = jnp.concatenate([-jnp.ones((half,), F32), jnp.ones((half,), F32)]).reshape(1, RET_DK)
    ret_out = _retention(q, k, v, gate, inv2, sgn, bsz, seqlen)

    ssd_out = _ssd(z, xbc, dt, ssd_conv_w.astype(F32), row(ssd_conv_b), _pad_lanes(ssd_dt_bias),
                   _pad_lanes(ssd_A_log), row(jnp.repeat(ssd_D.astype(F32), SSD_P)), row(ssd_norm_w),
                   _head_expander(SSD_HEADS, SSD_P), bsz, seqlen)

    x2d = _out_mlp(x2d, [ret_out, ssd_out], [bf(l0_w_out[:ret_w]), bf(l0_w_out[ret_w:])],
                   row(l0_norm_mlp), bf(l0_w_up), bf(l0_w_down))

    gdn_w = GDN_HEADS * GDN_DK
    offs, acc = [], 0
    for wdt in (3 * gdn_w, gdn_w, GDN_HEADS, GDN_HEADS, S5_CH):
        offs.append((acc, acc + wdt))
        acc += wdt
    w1 = [l1_w_in[:, a:b] for a, b in offs]
    w1[2] = _pad_cols(w1[2])
    w1[3] = _pad_cols(w1[3])
    qkv, zg, b_raw, a_raw, u = _norm_proj(x2d, row(l1_norm_mix), [bf(w) for w in w1])

    gdn_out = _gdn(qkv, zg, b_raw, a_raw, gdn_conv_w.astype(F32), _pad_lanes(gdn_A_log),
                   _pad_lanes(gdn_dt_bias), row(gdn_norm_w), _head_expander(GDN_HEADS, GDN_DK), bsz, seqlen)

    flat = lambda p: p.reshape(1, S5_GROUPS * S5_STATE).astype(F32)
    b_re = _block_diag(jnp.swapaxes(s5_B_re.astype(F32), 1, 2))
    b_im = _block_diag(jnp.swapaxes(s5_B_im.astype(F32), 1, 2))
    c_re = _block_diag(jnp.swapaxes(s5_C_re.astype(F32), 1, 2))
    c_im = _block_diag(jnp.swapaxes(s5_C_im.astype(F32), 1, 2))
    s5_out = _s5(u, flat(s5_A_re), flat(s5_A_im), row(jnp.repeat(s5_log_step.astype(F32), S5_STATE)),
                 bf(b_re), bf(b_im), bf(c_re), bf(c_im), row(s5_D), bf(s5_w_glu), row(s5_b_glu), bsz, seqlen)

    out = _out_mlp(x2d, [gdn_out, s5_out], [bf(l1_w_out[:gdn_w]), bf(l1_w_out[gdn_w:])],
                   row(l1_norm_mlp), bf(l1_w_up), bf(l1_w_down), final_gain=row(final_norm))
    return out.reshape(bsz, seqlen, d)
```

```python
import functools
import math

import jax
import jax.numpy as jnp
from jax import lax
from jax.experimental import pallas as pl
from jax.experimental.pallas import tpu as pltpu

F32 = jnp.float32
BF16 = jnp.bfloat16
EPS = 1e-6
CONV_WIDTH = 4
HALO = 8

RET_HEADS, RET_DK = 4, 128
SSD_HEADS, SSD_P, SSD_GROUPS, SSD_N = 8, 64, 2, 128
GDN_HEADS, GDN_DK = 6, 128
GDN_CHUNK = 64
S5_CH, S5_GROUPS, S5_GROUP, S5_STATE = 256, 16, 16, 64
ROPE_THETA = 10000.0
LANES = 128

VMEM_LIMIT = 56 * 1024 * 1024


def _full(shape):
    nd = len(shape)
    return pl.BlockSpec(shape, lambda *_: (0,) * nd)


def _resident(shape):
    nd = len(shape)
    return pl.BlockSpec(shape, lambda *_: (0,) * nd, pipeline_mode=pl.Buffered(1))


def _params(n_axes):
    return pltpu.CompilerParams(dimension_semantics=("arbitrary",) * n_axes, vmem_limit_bytes=VMEM_LIMIT)


def _mm(a, b):
    return jnp.dot(a.astype(BF16), b.astype(BF16), preferred_element_type=F32)


def _mm_nt(a, b):
    return lax.dot_general(a.astype(BF16), b.astype(BF16), (((1,), (1,)), ((), ())), preferred_element_type=F32)


def _bmm(a, b):
    return lax.dot_general(a.astype(BF16), b.astype(BF16), (((2,), (1,)), ((0,), (0,))), preferred_element_type=F32)


def _bmm_nt(a, b):
    return lax.dot_general(a.astype(BF16), b.astype(BF16), (((2,), (2,)), ((0,), (0,))), preferred_element_type=F32)


def _mm_hi(a, b):
    return jnp.dot(a, b, preferred_element_type=F32, precision=lax.Precision.HIGHEST)


def _sigmoid(x):
    return 1.0 / (1.0 + jnp.exp(-x))


def _silu(x):
    return x * _sigmoid(x)


def _softplus(x):
    return jnp.maximum(x, 0.0) + jnp.log(1.0 + jnp.exp(-jnp.abs(x)))


def _rms(x):
    return x * lax.rsqrt(jnp.mean(x * x, axis=-1, keepdims=True) + EPS)


def _norm_proj_body(x_ref, g_ref, *refs, n_out, sub_rows):
    w_refs, o_refs = refs[:n_out], refs[n_out:]
    for r0 in range(0, x_ref.shape[0], sub_rows):
        rs = slice(r0, r0 + sub_rows)
        xb = (_rms(x_ref[rs, :]) * g_ref[...]).astype(BF16)
        for w_ref, o_ref in zip(w_refs, o_refs):
            o_ref[rs, :] = jnp.dot(xb, w_ref[...], preferred_element_type=F32)


def _norm_proj(x2d, gain, weights, tm=512, sub_rows=256):
    t, d = x2d.shape
    n_out = len(weights)
    return pl.pallas_call(
        functools.partial(_norm_proj_body, n_out=n_out, sub_rows=sub_rows),
        grid=(t // tm,),
        in_specs=[pl.BlockSpec((tm, d), lambda i: (i, 0)), _full((1, d))] + [_resident(w.shape) for w in weights],
        out_specs=[pl.BlockSpec((tm, w.shape[1]), lambda i: (i, 0)) for w in weights],
        out_shape=[jax.ShapeDtypeStruct((t, w.shape[1]), F32) for w in weights],
        compiler_params=_params(1),
        name="norm_proj",
    )(x2d, gain, *weights)


def _out_mlp_body(x_ref, *refs, n_mix, ff_chunk, final, sub_rows):
    mix_refs, wo_refs = refs[:n_mix], refs[n_mix:2 * n_mix]
    rest = refs[2 * n_mix:]
    if final:
        g_ref, wu_ref, wd_ref, gf_ref, o_ref = rest
    else:
        g_ref, wu_ref, wd_ref, o_ref = rest
    d_ff = wu_ref.shape[1]
    for r0 in range(0, x_ref.shape[0], sub_rows):
        rs = slice(r0, r0 + sub_rows)
        x1 = x_ref[rs, :]
        for m_ref, w_ref in zip(mix_refs, wo_refs):
            x1 = x1 + jnp.dot(m_ref[rs, :].astype(BF16), w_ref[...], preferred_element_type=F32)
        xb = (_rms(x1) * g_ref[...]).astype(BF16)
        mlp = None
        for c in range(d_ff // ff_chunk):
            a = jnp.dot(xb, wu_ref[:, c * ff_chunk:(c + 1) * ff_chunk], preferred_element_type=F32)
            a = jnp.maximum(a, 0.0)
            t = jnp.dot((a * a).astype(BF16), wd_ref[c * ff_chunk:(c + 1) * ff_chunk, :],
                        preferred_element_type=F32)
            mlp = t if mlp is None else mlp + t
        acc = x1 + mlp
        if final:
            acc = _rms(acc) * gf_ref[...]
        o_ref[rs, :] = acc


def _out_mlp(x2d, mixes, w_outs, gain, w_up, w_down, final_gain=None, tm=512, sub_rows=512, ff_chunk=1024):
    t, d = x2d.shape
    n_mix = len(mixes)
    final = final_gain is not None
    row = lambda w: pl.BlockSpec((tm, w), lambda i: (i, 0))
    in_specs = [row(d)] + [row(m.shape[1]) for m in mixes] + [_resident(w.shape) for w in w_outs]
    in_specs += [_full((1, d)), _resident(w_up.shape), _resident(w_down.shape)]
    args = [x2d, *mixes, *w_outs, gain, w_up, w_down]
    if final:
        in_specs.append(_full((1, d)))
        args.append(final_gain)
    return pl.pallas_call(
        functools.partial(_out_mlp_body, n_mix=n_mix, ff_chunk=ff_chunk, final=final, sub_rows=sub_rows),
        grid=(t // tm,),
        in_specs=in_specs,
        out_specs=row(d),
        out_shape=jax.ShapeDtypeStruct((t, d), F32),
        compiler_params=_params(1),
        name="out_mlp",
    )(*args)


def _retention_body(q_ref, k_ref, v_ref, g_ref, inv_ref, sgn_ref, o_ref, s_ref, dm_ref, *, chunk):
    c = chunk
    i = pl.program_id(1)
    gammas = [math.log(1.0 - 2.0 ** (-5.0 - h)) for h in range(RET_HEADS)]

    @pl.when(i == 0)
    def _():
        s_ref[...] = jnp.zeros_like(s_ref)
        rowi = lax.broadcasted_iota(jnp.int32, (c, c), 0)
        coli = lax.broadcasted_iota(jnp.int32, (c, c), 1)
        diff = (rowi - coli).astype(F32)
        for h in range(RET_HEADS):
            dm_ref[h] = jnp.where(rowi >= coli, jnp.exp(gammas[h] * diff), 0.0)

    idx = lax.broadcasted_iota(jnp.int32, (c, 1), 0).astype(F32)
    pos = idx + (i * c).astype(F32)
    ang = pos * inv_ref[...]
    cos = jnp.cos(ang)
    sin = jnp.sin(ang) * sgn_ref[...]

    def rot(x):
        return x * cos + pltpu.roll(x, RET_DK // 2, 1) * sin

    for h in range(RET_HEADS):
        sl = slice(h * RET_DK, (h + 1) * RET_DK)
        lg = gammas[h]
        qh = rot(q_ref[:, sl]) * (RET_DK ** -0.5)
        kh = rot(k_ref[:, sl])
        vh = v_ref[:, sl]
        scores = _mm_nt(qh, kh) * dm_ref[h]
        state = s_ref[h]
        y = _mm(scores, vh) + _mm(qh * jnp.exp(lg * (idx + 1.0)), state)
        kw = kh * jnp.exp(lg * (c - 1.0 - idx))
        s_ref[h] = state * math.exp(lg * c) + _mm(kw.T, vh)
        gt = g_ref[:, sl]
        o_ref[:, sl] = _silu(gt) * _rms(y)


def _retention(q, k, v, gate, inv2, sgn, bsz, seqlen, chunk=256):
    nt = seqlen // chunk
    w = RET_HEADS * RET_DK
    row = pl.BlockSpec((chunk, w), lambda b, i: (b * nt + i, 0))
    return pl.pallas_call(
        functools.partial(_retention_body, chunk=chunk),
        grid=(bsz, nt),
        in_specs=[row, row, row, row, _full((1, RET_DK)), _full((1, RET_DK))],
        out_specs=row,
        out_shape=jax.ShapeDtypeStruct((bsz * seqlen, w), F32),
        scratch_shapes=[pltpu.VMEM((RET_HEADS, RET_DK, RET_DK), F32),
                        pltpu.VMEM((RET_HEADS, chunk, chunk), F32)],
        compiler_params=_params(2),
        name="retention",
    )(q, k, v, gate, inv2, sgn)


def _causal_conv(xe_ref, x_ref, w_ref, rows, first):
    @pl.when(first)
    def _():
        xe_ref[0:HALO, :] = jnp.zeros((HALO, xe_ref.shape[1]), F32)

    xe_ref[HALO:HALO + rows, :] = x_ref[...]
    acc = None
    for j in range(CONV_WIDTH):
        off = HALO - (CONV_WIDTH - 1) + j
        term = w_ref[j:j + 1, :] * xe_ref[off:off + rows, :]
        acc = term if acc is None else acc + term
    xe_ref[0:HALO, :] = xe_ref[rows:rows + HALO, :]
    return acc


def _ssd_body(z_ref, x_ref, dt_ref, cw_ref, cb_ref, dtb_ref, al_ref, de_ref, nw_ref, e8_ref,
              o_ref, xe_ref, s_ref, *, chunk):
    c = chunk
    i = pl.program_id(1)
    inner = SSD_HEADS * SSD_P
    gw = inner // SSD_GROUPS
    hpg = SSD_HEADS // SSD_GROUPS

    @pl.when(i == 0)
    def _():
        s_ref[...] = jnp.zeros_like(s_ref)

    xc = _silu(_causal_conv(xe_ref, x_ref, cw_ref, c, i == 0) + cb_ref[...])
    xs = xc[:, :inner]
    bm = xc[:, inner:inner + SSD_GROUPS * SSD_N]
    cm = xc[:, inner + SSD_GROUPS * SSD_N:]

    dtv = _softplus(dt_ref[...] + dtb_ref[...])
    la = dtv * (-jnp.exp(al_ref[...]))
    rowi = lax.broadcasted_iota(jnp.int32, (c, c), 0)
    coli = lax.broadcasted_iota(jnp.int32, (c, c), 1)
    causal = rowi >= coli
    cum = _mm_hi(causal.astype(F32), la)
    cum_t = cum.T
    e8 = e8_ref[...]
    dt_e = _mm_hi(dtv, e8)
    cum_e = _mm_hi(cum, e8)
    last_e = cum_e[c - 1:c, :]
    xd = xs * dt_e
    xdw = xd * jnp.exp(last_e - cum_e)
    lane = lax.broadcasted_iota(jnp.int32, (1, gw), 1)

    ys = []
    for g in range(SSD_GROUPS):
        bg = bm[:, g * SSD_N:(g + 1) * SSD_N]
        cg = cm[:, g * SSD_N:(g + 1) * SSD_N]
        gsl = slice(g * gw, (g + 1) * gw)
        cb = _mm_nt(cg, bg)
        xd_g = xd[:, gsl]
        yd = None
        for j in range(hpg):
            h = g * hpg + j
            seg = cum[:, h:h + 1] - cum_t[h:h + 1, :]
            lmat = jnp.exp(jnp.where(causal, seg, -jnp.inf))
            xm = jnp.where((lane >= SSD_P * j) & (lane < SSD_P * (j + 1)), xd_g, 0.0)
            t = _mm(cb * lmat, xm)
            yd = t if yd is None else yd + t
        state = s_ref[g]
        y_off = _mm(cg, state) * jnp.exp(cum_e[:, gsl])
        s_ref[g] = state * jnp.exp(last_e[:, gsl]) + _mm(bg.T, xdw[:, gsl])
        ys.append(yd + y_off)

    zz = z_ref[...]
    for g in range(SSD_GROUPS):
        gsl = slice(g * gw, (g + 1) * gw)
        yg = (ys[g] + de_ref[:, gsl] * xs[:, gsl]) * _silu(zz[:, gsl])
        o_ref[:, gsl] = _rms(yg) * nw_ref[:, gsl]


def _ssd(z, xbc, dt, conv_w, conv_b, dt_bias, a_log, d_e, norm_w, e8, bsz, seqlen, chunk=256):
    nt = seqlen // chunk
    inner = SSD_HEADS * SSD_P
    wx = xbc.shape[1]
    row = lambda w: pl.BlockSpec((chunk, w), lambda b, i: (b * nt + i, 0))
    return pl.pallas_call(
        functools.partial(_ssd_body, chunk=chunk),
        grid=(bsz, nt),
        in_specs=[row(inner), row(wx), row(LANES), _full(conv_w.shape), _full((1, wx)), _full((1, LANES)),
                  _full((1, LANES)), _full((1, inner)), _full((1, inner)), _full(e8.shape)],
        out_specs=row(inner),
        out_shape=jax.ShapeDtypeStruct((bsz * seqlen, inner), F32),
        scratch_shapes=[pltpu.VMEM((chunk + 2 * HALO, wx), F32),
                        pltpu.VMEM((SSD_GROUPS, SSD_N, inner // SSD_GROUPS), F32)],
        compiler_params=_params(2),
        name="ssd",
    )(z, xbc, dt, conv_w, conv_b, dt_bias, a_log, d_e, norm_w, e8)


def _gdn_body(qkv_ref, z_ref, b_ref, a_ref, cw_ref, al_ref, dtb_ref, nw_ref, e6_ref,
              o_ref, xe_ref, s_ref, *, tile):
    tt = tile
    ck = GDN_CHUNK
    nck = tt // ck
    i = pl.program_id(1)
    qk_w = GDN_HEADS * GDN_DK

    @pl.when(i == 0)
    def _():
        s_ref[...] = jnp.zeros_like(s_ref)

    xc = _silu(_causal_conv(xe_ref, qkv_ref, cw_ref, tt, i == 0))
    beta = _sigmoid(b_ref[...])
    gl = -jnp.exp(al_ref[...]) * _softplus(a_ref[...] + dtb_ref[...])

    rowi = lax.broadcasted_iota(jnp.int32, (tt, tt), 0)
    coli = lax.broadcasted_iota(jnp.int32, (tt, tt), 1)

    def same_block(bits):
        return (rowi >> bits) == (coli >> bits)

    blk = same_block(6)
    causal = blk & (rowi >= coli)
    strict = blk & (rowi > coli)
    gc = _mm_hi(causal.astype(F32), gl)
    gc_t = gc.T
    e6 = e6_ref[...]
    gc_e = _mm_hi(gc, e6)
    beta_e = _mm_hi(beta, e6)
    eg_e = jnp.exp(gc_e)
    glast_rows = [jnp.broadcast_to(gc_e[(cix + 1) * ck - 1:(cix + 1) * ck, :], (ck, qk_w)) for cix in range(nck)]
    glast_e = jnp.concatenate(glast_rows, axis=0)
    tail_e = jnp.exp(glast_e - gc_e)
    dlast_e = jnp.exp(glast_e)

    eye = (rowi == coli).astype(F32)
    base_mask = same_block(3) & (rowi > coli)
    level_masks = [same_block(b + 1) & (~same_block(b)) & (rowi > coli) for b in (3, 4, 5)]

    q_l, k_l, kb_l, rhs_l, kt_l, decay_l = [], [], [], [], [], []
    for h in range(GDN_HEADS):
        sl = slice(h * GDN_DK, (h + 1) * GDN_DK)
        qh = xc[:, sl]
        kh = xc[:, qk_w + h * GDN_DK: qk_w + (h + 1) * GDN_DK]
        vh = xc[:, 2 * qk_w + h * GDN_DK: 2 * qk_w + (h + 1) * GDN_DK]
        qh = qh * lax.rsqrt(jnp.sum(qh * qh, axis=-1, keepdims=True) + EPS) * (GDN_DK ** -0.5)
        kh = kh * lax.rsqrt(jnp.sum(kh * kh, axis=-1, keepdims=True) + EPS)
        bh = beta_e[:, sl]
        kb = kh * bh
        seg = gc[:, h:h + 1] - gc_t[h:h + 1, :]
        q_l.append(qh)
        k_l.append(kh)
        kb_l.append(kb)
        rhs_l.append(jnp.concatenate([vh * bh, kb * eg_e[:, sl]], axis=1))
        kt_l.append(kh * tail_e[:, sl])
        decay_l.append(jnp.exp(jnp.where(causal, seg, -jnp.inf)))
    q3, k3, kb3 = jnp.stack(q_l), jnp.stack(k_l), jnp.stack(kb_l)
    decay = jnp.stack(decay_l)
    lower = jnp.where(strict, _bmm_nt(kb3, k3) * decay, 0.0)
    attn = _bmm_nt(q3, k3) * decay

    d8 = jnp.where(base_mask, lower, 0.0)
    x2 = _bmm(d8, d8)
    x4 = _bmm(x2, x2)
    tinv = eye - d8
    tinv = tinv + _bmm(tinv, x2)
    tinv = tinv + _bmm(tinv, x4)
    for lm in level_masks:
        e = jnp.where(lm, lower, 0.0)
        tinv = tinv - _bmm(tinv, _bmm(e, tinv))

    uw = _bmm(tinv, jnp.stack(rhs_l))
    u, w = uw[:, :, :GDN_DK], uw[:, :, GDN_DK:]
    qg = jnp.stack([q_l[h] * eg_e[:, h * GDN_DK:(h + 1) * GDN_DK] for h in range(GDN_HEADS)])
    dl = jnp.stack([dlast_e[:, h * GDN_DK:(h + 1) * GDN_DK] for h in range(GDN_HEADS)])

    state = s_ref[...]
    v_news, outs = [], []
    for cix in range(nck):
        rs = slice(cix * ck, (cix + 1) * ck)
        r = _bmm(jnp.concatenate([w[:, rs], qg[:, rs]], axis=1), state)
        v_new = u[:, rs] - r[:, :ck]
        v_news.append(v_new)
        pad = [jnp.zeros((GDN_HEADS, ck, GDN_DK), F32)] * (nck - 1 - cix)
        outs.append(r[:, ck:] + _bmm(attn[:, rs], jnp.concatenate(v_news + pad, axis=1)))
        kt_t = jnp.stack([kt_l[h][rs].T for h in range(GDN_HEADS)])
        state = state * dl[:, cix * ck:cix * ck + 1] + _bmm(kt_t, v_new)
    s_ref[...] = state
    o = jnp.concatenate(outs, axis=1)
    for h in range(GDN_HEADS):
        sl = slice(h * GDN_DK, (h + 1) * GDN_DK)
        o_ref[:, sl] = _rms(o[h]) * nw_ref[...] * _silu(z_ref[:, sl])


def _gdn(qkv, z, b_raw, a_raw, conv_w, a_log, dt_bias, norm_w, e6, bsz, seqlen, tile=256):
    nt = seqlen // tile
    wq = qkv.shape[1]
    wv = GDN_HEADS * GDN_DK
    row = lambda w: pl.BlockSpec((tile, w), lambda b, i: (b * nt + i, 0))
    return pl.pallas_call(
        functools.partial(_gdn_body, tile=tile),
        grid=(bsz, nt),
        in_specs=[row(wq), row(wv), row(LANES), row(LANES), _full(conv_w.shape), _full((1, LANES)),
                  _full((1, LANES)), _full((1, GDN_DK)), _full(e6.shape)],
        out_specs=row(wv),
        out_shape=jax.ShapeDtypeStruct((bsz * seqlen, wv), F32),
        scratch_shapes=[pltpu.VMEM((tile + 2 * HALO, wq), F32),
                        pltpu.VMEM((GDN_HEADS, GDN_DK, GDN_DK), F32)],
        compiler_params=_params(2),
        name="gdn",
    )(qkv, z, b_raw, a_raw, conv_w, a_log, dt_bias, norm_w, e6)


def _cmul_add(ar, ai, br, bi, cr, ci):
    return ar * br - ai * bi + cr, ar * bi + ai * br + ci


def _s5_body(u_ref, are_ref, aim_ref, ls_ref, bre_ref, bim_ref, cre_ref, cim_ref, d_ref, wg_ref, bg_ref,
             o_ref, pre_ref, pim_ref, qre_ref, qim_ref, h_ref, hre_ref, him_ref, *, tile):
    tt = tile
    sub = 8
    seg = tt // sub
    seg_bits = seg.bit_length() - 1
    ns = S5_GROUPS * S5_STATE
    b = pl.program_id(0)
    i = pl.program_id(1)

    @pl.when((b == 0) & (i == 0))
    def _():
        step = jnp.exp(ls_ref[...])

        def powers(n, scale):
            tpos = (lax.broadcasted_iota(jnp.int32, (n, 1), 0).astype(F32) + 1.0) * scale
            mag = jnp.exp(tpos * (are_ref[...] * step))
            ang = tpos * (aim_ref[...] * step)
            return mag * jnp.cos(ang), mag * jnp.sin(ang)

        pre_ref[...], pim_ref[...] = powers(seg, 1.0)
        qre_ref[...], qim_ref[...] = powers(sub, float(seg))

    @pl.when(i == 0)
    def _():
        h_ref[...] = jnp.zeros_like(h_ref)

    row_i = lax.broadcasted_iota(jnp.int32, (tt, tt), 0)
    col_i = lax.broadcasted_iota(jnp.int32, (tt, tt), 1)
    perm = (col_i == (row_i & (sub - 1)) * seg + (row_i >> 3)).astype(F32)
    unperm = (col_i == (row_i & (seg - 1)) * sub + (row_i >> seg_bits)).astype(F32)
    u = _mm_hi(perm, u_ref[...])

    bu_re = _mm(u, bre_ref[...])
    bu_im = _mm(u, bim_ref[...])
    a_re, a_im = are_ref[...], aim_ref[...]
    n_re, n_im = pre_ref[0:1, :] - 1.0, pim_ref[0:1, :]
    den = a_re * a_re + a_im * a_im
    c_re = (n_re * a_re + n_im * a_im) / den
    c_im = (n_im * a_re - n_re * a_im) / den
    hre_ref[...] = c_re * bu_re - c_im * bu_im
    him_ref[...] = c_re * bu_im + c_im * bu_re

    sl_i = lax.broadcasted_iota(jnp.int32, (sub, LANES), 0)

    def down(x, d, fill):
        return jnp.where(sl_i >= d, pltpu.roll(x, d, 0), fill)

    for j in range(ns // LANES):
        ls = slice(j * LANES, (j + 1) * LANES)
        ar, ai = pre_ref[0:1, ls], pim_ref[0:1, ls]
        gr, gi = hre_ref[0:sub, ls], him_ref[0:sub, ls]
        for t in range(1, seg):
            rs = slice(t * sub, (t + 1) * sub)
            gr, gi = _cmul_add(ar, ai, gr, gi, hre_ref[rs, ls], him_ref[rs, ls])
            hre_ref[rs, ls] = gr
            him_ref[rs, ls] = gi
        fr, fi = gr, gi
        for k in range(3):
            d = 1 << k
            mr, mi = qre_ref[d - 1:d, ls], qim_ref[d - 1:d, ls]
            fr, fi = _cmul_add(mr, mi, down(fr, d, 0.0), down(fi, d, 0.0), fr, fi)
        cr, ci = h_ref[0:1, ls], h_ref[1:2, ls]
        zr, zi = _cmul_add(qre_ref[:, ls], qim_ref[:, ls], cr, ci, fr, fi)
        h_ref[0:1, ls] = zr[sub - 1:sub, :]
        h_ref[1:2, ls] = zi[sub - 1:sub, :]
        er, ei = down(zr, 1, cr), down(zi, 1, ci)
        for t in range(seg):
            rs = slice(t * sub, (t + 1) * sub)
            hr, hi = _cmul_add(pre_ref[t:t + 1, ls], pim_ref[t:t + 1, ls], er, ei, hre_ref[rs, ls], him_ref[rs, ls])
            hre_ref[rs, ls] = hr
            him_ref[rs, ls] = hi

    y = _mm(hre_ref[...], cre_ref[...]) - _mm(him_ref[...], cim_ref[...]) + d_ref[...] * u
    y = jax.nn.gelu(y)
    o_ref[...] = _mm_hi(unperm, y * _sigmoid(_mm(y, wg_ref[...]) + bg_ref[...]))


def _s5(u, a_re, a_im, log_step, b_re, b_im, c_re, c_im, d_skip, w_glu, b_glu, bsz, seqlen, tile=256):
    nt = seqlen // tile
    ns = S5_GROUPS * S5_STATE
    row = pl.BlockSpec((tile, S5_CH), lambda b, i: (b * nt + i, 0))
    vec = _full((1, ns))
    return pl.pallas_call(
        functools.partial(_s5_body, tile=tile),
        grid=(bsz, nt),
        in_specs=[row, vec, vec, vec, _full((S5_CH, ns)), _full((S5_CH, ns)), _full((ns, S5_CH)),
                  _full((ns, S5_CH)), _full((1, S5_CH)), _full((S5_CH, S5_CH)), _full((1, S5_CH))],
        out_specs=row,
        out_shape=jax.ShapeDtypeStruct((bsz * seqlen, S5_CH), F32),
        scratch_shapes=[pltpu.VMEM((tile // 8, ns), F32), pltpu.VMEM((tile // 8, ns), F32),
                        pltpu.VMEM((8, ns), F32), pltpu.VMEM((8, ns), F32), pltpu.VMEM((8, ns), F32),
                        pltpu.VMEM((tile, ns), F32), pltpu.VMEM((tile, ns), F32)],
        compiler_params=_params(2),
        name="s5",
    )(u, a_re, a_im, log_step, b_re, b_im, c_re, c_im, d_skip, w_glu, b_glu)


def _pad_lanes(v, width=LANES):
    v = v.reshape(1, -1).astype(F32)
    return jnp.pad(v, ((0, 0), (0, width - v.shape[1])))


def _pad_cols(w, width=LANES):
    return jnp.pad(w, ((0, 0), (0, width - w.shape[1])))


def _head_expander(n_heads, head_width, rows=LANES):
    r = jnp.arange(rows)[:, None]
    col_head = jnp.arange(n_heads * head_width)[None, :] // head_width
    return (r == col_head).astype(F32)


def _block_diag(blocks):
    g, r, c = blocks.shape
    eye = jnp.eye(g, dtype=blocks.dtype)
    return (eye[:, None, :, None] * blocks[:, :, None, :]).reshape(g * r, g * c)


def kernel(x, l0_norm_mix, l0_w_in, ssd_conv_w, ssd_conv_b, ssd_dt_bias, ssd_A_log, ssd_D, ssd_norm_w, l0_w_out, l0_norm_mlp, l0_w_up, l0_w_down, l1_norm_mix, l1_w_in, gdn_conv_w, gdn_A_log, gdn_dt_bias, gdn_norm_w, s5_A_re, s5_A_im, s5_log_step, s5_B_re, s5_B_im, s5_C_re, s5_C_im, s5_D, s5_w_glu, s5_b_glu, l1_w_out, l1_norm_mlp, l1_w_up, l1_w_down, final_norm):
    bsz, seqlen, d = x.shape
    x2d = x.reshape(bsz * seqlen, d)
    row = lambda v: v.reshape(1, -1).astype(F32)
    bf = lambda w: w.astype(BF16)

    ret_w = RET_HEADS * RET_DK
    ssd_inner = SSD_HEADS * SSD_P
    ssd_xbc = ssd_inner + 2 * SSD_GROUPS * SSD_N
    offs, acc = [], 0
    for wdt in (ret_w, ret_w, ret_w, ret_w, ssd_inner, ssd_xbc, SSD_HEADS):
        offs.append((acc, acc + wdt))
        acc += wdt
    w0 = [l0_w_in[:, a:b] for a, b in offs]
    w0[-1] = _pad_cols(w0[-1])
    q, k, v, gate, z, xbc, dt = _norm_proj(x2d, row(l0_norm_mix), [bf(w) for w in w0])

    half = RET_DK // 2
    inv = ROPE_THETA ** (-jnp.arange(half, dtype=F32) / half)
    inv2 = jnp.concatenate([inv, inv]).reshape(1, RET_DK)
    sgn = jnp.concatenate([-jnp.ones((half,), F32), jnp.ones((half,), F32)]).reshape(1, RET_DK)
    ret_out = _retention(q, k, v, gate, inv2, sgn, bsz, seqlen)

    ssd_out = _ssd(z, xbc, dt, ssd_conv_w.astype(F32), row(ssd_conv_b), _pad_lanes(ssd_dt_bias),
                   _pad_lanes(ssd_A_log), row(jnp.repeat(ssd_D.astype(F32), SSD_P)), row(ssd_norm_w),
                   _head_expander(SSD_HEADS, SSD_P), bsz, seqlen)

    x2d = _out_mlp(x2d, [ret_out, ssd_out], [bf(l0_w_out[:ret_w]), bf(l0_w_out[ret_w:])],
                   row(l0_norm_mlp), bf(l0_w_up), bf(l0_w_down))

    gdn_w = GDN_HEADS * GDN_DK
    offs, acc = [], 0
    for wdt in (3 * gdn_w, gdn_w, GDN_HEADS, GDN_HEADS, S5_CH):
        offs.append((acc, acc + wdt))
        acc += wdt
    w1 = [l1_w_in[:, a:b] for a, b in offs]
    w1[2] = _pad_cols(w1[2])
    w1[3] = _pad_cols(w1[3])
    qkv, zg, b_raw, a_raw, u = _norm_proj(x2d, row(l1_norm_mix), [bf(w) for w in w1])

    gdn_out = _gdn(qkv, zg, b_raw, a_raw, gdn_conv_w.astype(F32), _pad_lanes(gdn_A_log),
                   _pad_lanes(gdn_dt_bias), row(gdn_norm_w), _head_expander(GDN_HEADS, GDN_DK), bsz, seqlen)

    flat = lambda p: p.reshape(1, S5_GROUPS * S5_STATE).astype(F32)
    b_re = _block_diag(jnp.swapaxes(s5_B_re.astype(F32), 1, 2))
    b_im = _block_diag(jnp.swapaxes(s5_B_im.astype(F32), 1, 2))
    c_re = _block_diag(jnp.swapaxes(s5_C_re.astype(F32), 1, 2))
    c_im = _block_diag(jnp.swapaxes(s5_C_im.astype(F32), 1, 2))
    s5_out = _s5(u, flat(s5_A_re), flat(s5_A_im), row(jnp.repeat(s5_log_step.astype(F32), S5_STATE)),
                 bf(b_re), bf(b_im), bf(c_re), bf(c_im), row(s5_D), bf(s5_w_glu), row(s5_b_glu), bsz, seqlen)

    out = _out_mlp(x2d, [gdn_out, s5_out], [bf(l1_w_out[:gdn_w]), bf(l1_w_out[gdn_w:])],
                   row(l1_norm_mlp), bf(l1_w_up), bf(l1_w_down), final_gain=row(final_norm))
    return out.reshape(bsz, seqlen, d)
```

```python
import functools
import math

import jax
import jax.numpy as jnp
from jax import lax
from jax.experimental import pallas as pl
from jax.experimental.pallas import tpu as pltpu

F32 = jnp.float32
BF16 = jnp.bfloat16
EPS = 1e-6
CONV_WIDTH = 4
HALO = 8

RET_HEADS, RET_DK = 4, 128
SSD_HEADS, SSD_P, SSD_GROUPS, SSD_N = 8, 64, 2, 128
GDN_HEADS, GDN_DK = 6, 128
GDN_CHUNK = 64
S5_CH, S5_GROUPS, S5_GROUP, S5_STATE = 256, 16, 16, 64
ROPE_THETA = 10000.0
LANES = 128

VMEM_LIMIT = 56 * 1024 * 1024


def _full(shape):
    nd = len(shape)
    return pl.BlockSpec(shape, lambda *_: (0,) * nd)


def _resident(shape):
    nd = len(shape)
    return pl.BlockSpec(shape, lambda *_: (0,) * nd, pipeline_mode=pl.Buffered(1))


def _params(n_axes):
    return pltpu.CompilerParams(dimension_semantics=("arbitrary",) * n_axes, vmem_limit_bytes=VMEM_LIMIT)


def _mm(a, b):
    return jnp.dot(a.astype(BF16), b.astype(BF16), preferred_element_type=F32)


def _mm_nt(a, b):
    return lax.dot_general(a.astype(BF16), b.astype(BF16), (((1,), (1,)), ((), ())), preferred_element_type=F32)


def _bmm(a, b):
    return lax.dot_general(a.astype(BF16), b.astype(BF16), (((2,), (1,)), ((0,), (0,))), preferred_element_type=F32)


def _bmm_nt(a, b):
    return lax.dot_general(a.astype(BF16), b.astype(BF16), (((2,), (2,)), ((0,), (0,))), preferred_element_type=F32)


def _split3(a):
    hi = a.astype(BF16)
    r1 = a - hi.astype(F32)
    mid = r1.astype(BF16)
    lo = (r1 - mid.astype(F32)).astype(BF16)
    return hi, mid, lo


def _rows01(m01, a):
    n = a.shape[1]
    p = jnp.dot(m01.astype(BF16), jnp.concatenate(_split3(a), axis=1), preferred_element_type=F32)
    return p[:, :n] + p[:, n:2 * n] + p[:, 2 * n:]


def _cols01(a, e01):
    hi, mid, lo = _split3(a)
    dot = lambda x: jnp.dot(x, e01, preferred_element_type=F32)
    return dot(hi) + dot(mid) + dot(lo)


def _sigmoid(x):
    return 1.0 / (1.0 + jnp.exp(-x))


def _silu(x):
    return x * _sigmoid(x)


def _softplus(x):
    return jnp.maximum(x, 0.0) + jnp.log(1.0 + jnp.exp(-jnp.abs(x)))


def _rms(x):
    return x * lax.rsqrt(jnp.mean(x * x, axis=-1, keepdims=True) + EPS)


def _norm_proj_body(*refs, kinds, tiles_per_seq):
    it = iter(refs)
    x_ref, g_ref = next(it), next(it)
    w_refs = [next(it) for _ in kinds]
    has_conv = any(k.startswith("conv") for k in kinds)
    has_rope = any(k.startswith("rope") for k in kinds)
    cw_ref = next(it) if has_conv else None
    cb_ref = next(it) if "conv_bias" in kinds else None
    inv_ref, sgn_ref = (next(it), next(it)) if has_rope else (None, None)
    o_refs = [next(it) for _ in kinds]
    xe_ref = next(it) if has_conv else None
    rc_ref, rs_ref = (next(it), next(it)) if has_rope else (None, None)

    tm = x_ref.shape[0]
    t_in_seq = pl.program_id(0) % tiles_per_seq

    if has_conv:
        @pl.when(t_in_seq == 0)
        def _():
            xe_ref[:, 0:HALO, :] = jnp.zeros((xe_ref.shape[0], HALO, LANES), F32)

    if has_rope:
        @pl.when(pl.program_id(0) == 0)
        def _():
            ang_r = lax.broadcasted_iota(jnp.int32, (tm, 1), 0).astype(F32) * inv_ref[...]
            rc_ref[...] = jnp.cos(ang_r)
            rs_ref[...] = jnp.sin(ang_r)

        ang_0 = (t_in_seq * tm).astype(F32) * inv_ref[...]
        c0, s0 = jnp.cos(ang_0), jnp.sin(ang_0)
        cos = rc_ref[...] * c0 - rs_ref[...] * s0
        sin = (rs_ref[...] * c0 + rc_ref[...] * s0) * sgn_ref[...]

    xb = (_rms(x_ref[...]) * g_ref[...]).astype(BF16)
    conv_col = 0
    for kind, w_ref, o_ref in zip(kinds, w_refs, o_refs):
        y = jnp.dot(xb, w_ref[...], preferred_element_type=F32)
        if kind.startswith("rope"):
            scale = RET_DK ** -0.5 if kind == "rope_q" else 1.0
            for h in range(y.shape[1] // RET_DK):
                sl = slice(h * RET_DK, (h + 1) * RET_DK)
                yh = y[:, sl]
                o_ref[:, sl] = (yh * cos + pltpu.roll(yh, RET_DK // 2, 1) * sin) * scale
        elif kind.startswith("conv"):
            for lt in range(y.shape[1] // LANES):
                ct = conv_col // LANES + lt
                ls = slice(conv_col + lt * LANES, conv_col + (lt + 1) * LANES)
                xe_ref[ct, HALO:HALO + tm, :] = y[:, lt * LANES:(lt + 1) * LANES]
                acc = None
                for j in range(CONV_WIDTH):
                    off = HALO - (CONV_WIDTH - 1) + j
                    term = cw_ref[j:j + 1, ls] * xe_ref[ct, off:off + tm, :]
                    acc = term if acc is None else acc + term
                if kind == "conv_bias":
                    acc = acc + cb_ref[:, ls]
                o_ref[:, lt * LANES:(lt + 1) * LANES] = _silu(acc)
                xe_ref[ct, 0:HALO, :] = xe_ref[ct, tm:tm + HALO, :]
            conv_col += y.shape[1]
        else:
            o_ref[...] = y


def _norm_proj(x2d, gain, weights, kinds, seqlen, conv_w=None, conv_b=None, rope=None, tm=512):
    t, d = x2d.shape
    in_specs = [pl.BlockSpec((tm, d), lambda i: (i, 0)), _full((1, d))] + [_resident(w.shape) for w in weights]
    args = [x2d, gain, *weights]
    scratch = []
    if conv_w is not None:
        in_specs.append(_full(conv_w.shape))
        args.append(conv_w)
        if conv_b is not None:
            in_specs.append(_full(conv_b.shape))
            args.append(conv_b)
        scratch.append(pltpu.VMEM((conv_w.shape[1] // LANES, tm + HALO, LANES), F32))
    if rope is not None:
        in_specs += [_full(r.shape) for r in rope]
        args += list(rope)
        scratch += [pltpu.VMEM((tm, RET_DK), F32), pltpu.VMEM((tm, RET_DK), F32)]
    return pl.pallas_call(
        functools.partial(_norm_proj_body, kinds=tuple(kinds), tiles_per_seq=seqlen // tm),
        grid=(t // tm,),
        in_specs=in_specs,
        out_specs=[pl.BlockSpec((tm, w.shape[1]), lambda i: (i, 0)) for w in weights],
        out_shape=[jax.ShapeDtypeStruct((t, w.shape[1]), F32) for w in weights],
        scratch_shapes=scratch,
        compiler_params=_params(1),
        name="norm_proj",
    )(*args)


def _out_mlp_body(x_ref, *refs, n_mix, ff_chunk, final):
    mix_refs, wo_refs = refs[:n_mix], refs[n_mix:2 * n_mix]
    rest = refs[2 * n_mix:]
    if final:
        g_ref, wu_ref, wd_ref, gf_ref, o_ref = rest
    else:
        g_ref, wu_ref, wd_ref, o_ref = rest
    d_ff = wu_ref.shape[1]
    x1 = x_ref[...]
    for m_ref, w_ref in zip(mix_refs, wo_refs):
        x1 = x1 + jnp.dot(m_ref[...].astype(BF16), w_ref[...], preferred_element_type=F32)
    xb = (_rms(x1) * g_ref[...]).astype(BF16)
    mlp = None
    for c in range(d_ff // ff_chunk):
        a = jnp.dot(xb, wu_ref[:, c * ff_chunk:(c + 1) * ff_chunk], preferred_element_type=F32)
        a = jnp.maximum(a, 0.0)
        t = jnp.dot((a * a).astype(BF16), wd_ref[c * ff_chunk:(c + 1) * ff_chunk, :],
                    preferred_element_type=F32)
        mlp = t if mlp is None else mlp + t
    acc = x1 + mlp
    if final:
        acc = _rms(acc) * gf_ref[...]
    o_ref[...] = acc


def _out_mlp(x2d, mixes, w_outs, gain, w_up, w_down, final_gain=None, tm=512, ff_chunk=1024):
    t, d = x2d.shape
    n_mix = len(mixes)
    final = final_gain is not None
    row = lambda w: pl.BlockSpec((tm, w), lambda i: (i, 0))
    in_specs = [row(d)] + [row(m.shape[1]) for m in mixes] + [_resident(w.shape) for w in w_outs]
    in_specs += [_full((1, d)), _resident(w_up.shape), _resident(w_down.shape)]
    args = [x2d, *mixes, *w_outs, gain, w_up, w_down]
    if final:
        in_specs.append(_full((1, d)))
        args.append(final_gain)
    return pl.pallas_call(
        functools.partial(_out_mlp_body, n_mix=n_mix, ff_chunk=ff_chunk, final=final),
        grid=(t // tm,),
        in_specs=in_specs,
        out_specs=row(d),
        out_shape=jax.ShapeDtypeStruct((t, d), F32),
        compiler_params=_params(1),
        name="out_mlp",
    )(*args)


def _retention_body(q_ref, k_ref, v_ref, g_ref, o_ref, s_ref, dm_ref, *, chunk):
    c = chunk
    i = pl.program_id(1)
    gammas = [math.log(1.0 - 2.0 ** (-5.0 - h)) for h in range(RET_HEADS)]

    @pl.when(i == 0)
    def _():
        s_ref[...] = jnp.zeros_like(s_ref)
        rowi = lax.broadcasted_iota(jnp.int32, (c, c), 0)
        coli = lax.broadcasted_iota(jnp.int32, (c, c), 1)
        diff = (rowi - coli).astype(F32)
        for h in range(RET_HEADS):
            dm_ref[h] = jnp.where(rowi >= coli, jnp.exp(gammas[h] * diff), 0.0)

    idx = lax.broadcasted_iota(jnp.int32, (c, 1), 0).astype(F32)
    for h in range(RET_HEADS):
        sl = slice(h * RET_DK, (h + 1) * RET_DK)
        lg = gammas[h]
        qh = q_ref[:, sl]
        kh = k_ref[:, sl]
        vh = v_ref[:, sl]
        scores = _mm_nt(qh, kh) * dm_ref[h]
        state = s_ref[h]
        y = _mm(scores, vh) + _mm(qh * jnp.exp(lg * (idx + 1.0)), state)
        kw = kh * jnp.exp(lg * (c - 1.0 - idx))
        s_ref[h] = state * math.exp(lg * c) + _mm(kw.T, vh)
        gt = g_ref[:, sl]
        o_ref[:, sl] = _silu(gt) * _rms(y)


def _retention(q, k, v, gate, bsz, seqlen, chunk=256):
    nt = seqlen // chunk
    w = RET_HEADS * RET_DK
    row = pl.BlockSpec((chunk, w), lambda b, i: (b * nt + i, 0))
    return pl.pallas_call(
        functools.partial(_retention_body, chunk=chunk),
        grid=(bsz, nt),
        in_specs=[row, row, row, row],
        out_specs=row,
        out_shape=jax.ShapeDtypeStruct((bsz * seqlen, w), F32),
        scratch_shapes=[pltpu.VMEM((RET_HEADS, RET_DK, RET_DK), F32),
                        pltpu.VMEM((RET_HEADS, chunk, chunk), F32)],
        compiler_params=_params(2),
        name="retention",
    )(q, k, v, gate)


def _ssd_body(z_ref, xs_ref, bc_ref, dt_ref, dtb_ref, al_ref, de_ref, nw_ref, e8_ref, o_ref, s_ref, *, chunk):
    c = chunk
    i = pl.program_id(1)
    inner = SSD_HEADS * SSD_P
    gw = inner // SSD_GROUPS
    hpg = SSD_HEADS // SSD_GROUPS

    @pl.when(i == 0)
    def _():
        s_ref[...] = jnp.zeros_like(s_ref)

    xs = xs_ref[...]
    bm = bc_ref[:, :SSD_GROUPS * SSD_N]
    cm = bc_ref[:, SSD_GROUPS * SSD_N:]

    dtv = _softplus(dt_ref[...] + dtb_ref[...])
    la = dtv * (-jnp.exp(al_ref[...]))
    rowi = lax.broadcasted_iota(jnp.int32, (c, c), 0)
    coli = lax.broadcasted_iota(jnp.int32, (c, c), 1)
    causal = rowi >= coli
    cum = _rows01(causal, la)
    cum_t = cum.T
    e8 = e8_ref[...]
    dt_e = _cols01(dtv, e8)
    cum_e = _cols01(cum, e8)
    last_e = cum_e[c - 1:c, :]
    xd = xs * dt_e
    xdw = xd * jnp.exp(last_e - cum_e)
    lane = lax.broadcasted_iota(jnp.int32, (1, gw), 1)

    ys = []
    for g in range(SSD_GROUPS):
        bg = bm[:, g * SSD_N:(g + 1) * SSD_N]
        cg = cm[:, g * SSD_N:(g + 1) * SSD_N]
        gsl = slice(g * gw, (g + 1) * gw)
        cb = _mm_nt(cg, bg)
        xd_g = xd[:, gsl]
        yd = None
        for j in range(hpg):
            h = g * hpg + j
            seg = cum[:, h:h + 1] - cum_t[h:h + 1, :]
            lmat = jnp.exp(jnp.where(causal, seg, -jnp.inf))
            xm = jnp.where((lane >= SSD_P * j) & (lane < SSD_P * (j + 1)), xd_g, 0.0)
            t = _mm(cb * lmat, xm)
            yd = t if yd is None else yd + t
        state = s_ref[g]
        y_off = _mm(cg, state) * jnp.exp(cum_e[:, gsl])
        s_ref[g] = state * jnp.exp(last_e[:, gsl]) + _mm(bg.T, xdw[:, gsl])
        ys.append(yd + y_off)

    zz = z_ref[...]
    for g in range(SSD_GROUPS):
        gsl = slice(g * gw, (g + 1) * gw)
        yg = (ys[g] + de_ref[:, gsl] * xs[:, gsl]) * _silu(zz[:, gsl])
        o_ref[:, gsl] = _rms(yg) * nw_ref[:, gsl]


def _ssd(z, xs, bc, dt, dt_bias, a_log, d_e, norm_w, e8, bsz, seqlen, chunk=256):
    nt = seqlen // chunk
    inner = SSD_HEADS * SSD_P
    row = lambda w: pl.BlockSpec((chunk, w), lambda b, i: (b * nt + i, 0))
    return pl.pallas_call(
        functools.partial(_ssd_body, chunk=chunk),
        grid=(bsz, nt),
        in_specs=[row(inner), row(inner), row(bc.shape[1]), row(LANES), _full((1, LANES)), _full((1, LANES)),
                  _full((1, inner)), _full((1, inner)), _full(e8.shape)],
        out_specs=row(inner),
        out_shape=jax.ShapeDtypeStruct((bsz * seqlen, inner), F32),
        scratch_shapes=[pltpu.VMEM((SSD_GROUPS, SSD_N, inner // SSD_GROUPS), F32)],
        compiler_params=_params(2),
        name="ssd",
    )(z, xs, bc, dt, dt_bias, a_log, d_e, norm_w, e8)


def _gdn_body(q_ref, k_ref, v_ref, z_ref, b_ref, a_ref, al_ref, dtb_ref, nw_ref, e6_ref, o_ref, s_ref, *, tile):
    tt = tile
    ck = GDN_CHUNK
    nck = tt // ck
    i = pl.program_id(1)
    qk_w = GDN_HEADS * GDN_DK

    @pl.when(i == 0)
    def _():
        s_ref[...] = jnp.zeros_like(s_ref)

    beta = _sigmoid(b_ref[...])
    gl = -jnp.exp(al_ref[...]) * _softplus(a_ref[...] + dtb_ref[...])

    rowi = lax.broadcasted_iota(jnp.int32, (tt, tt), 0)
    coli = lax.broadcasted_iota(jnp.int32, (tt, tt), 1)

    def same_block(bits):
        return (rowi >> bits) == (coli >> bits)

    blk = same_block(6)
    causal = blk & (rowi >= coli)
    strict = blk & (rowi > coli)
    gc = _rows01(causal, gl)
    gc_t = gc.T
    e6 = e6_ref[...]
    gc_e = _cols01(gc, e6)
    beta_e = _cols01(beta, e6)
    eg_e = jnp.exp(gc_e)
    glast_rows = [jnp.broadcast_to(gc_e[(cix + 1) * ck - 1:(cix + 1) * ck, :], (ck, qk_w)) for cix in range(nck)]
    glast_e = jnp.concatenate(glast_rows, axis=0)
    tail_e = jnp.exp(glast_e - gc_e)
    dlast_e = jnp.exp(glast_e)

    eye = (rowi == coli).astype(F32)
    base_mask = same_block(3) & (rowi > coli)
    level_masks = [same_block(b + 1) & (~same_block(b)) & (rowi > coli) for b in (3, 4, 5)]

    q_l, k_l, kb_l, rhs_l, kt_l, decay_l = [], [], [], [], [], []
    for h in range(GDN_HEADS):
        sl = slice(h * GDN_DK, (h + 1) * GDN_DK)
        qh = q_ref[:, sl]
        kh = k_ref[:, sl]
        vh = v_ref[:, sl]
        qh = qh * lax.rsqrt(jnp.sum(qh * qh, axis=-1, keepdims=True) + EPS) * (GDN_DK ** -0.5)
        kh = kh * lax.rsqrt(jnp.sum(kh * kh, axis=-1, keepdims=True) + EPS)
        bh = beta_e[:, sl]
        kb = kh * bh
        seg = gc[:, h:h + 1] - gc_t[h:h + 1, :]
        q_l.append(qh)
        k_l.append(kh)
        kb_l.append(kb)
        rhs_l.append(jnp.concatenate([vh * bh, kb * eg_e[:, sl]], axis=1))
        kt_l.append(kh * tail_e[:, sl])
        decay_l.append(jnp.exp(jnp.where(causal, seg, -jnp.inf)))
    q3, k3, kb3 = jnp.stack(q_l), jnp.stack(k_l), jnp.stack(kb_l)
    decay = jnp.stack(decay_l)
    lower = jnp.where(strict, _bmm_nt(kb3, k3) * decay, 0.0)
    attn = _bmm_nt(q3, k3) * decay

    d8 = jnp.where(base_mask, lower, 0.0)
    x2 = _bmm(d8, d8)
    x4 = _bmm(x2, x2)
    tinv = eye - d8
    tinv = tinv + _bmm(tinv, x2)
    tinv = tinv + _bmm(tinv, x4)
    for lm in level_masks:
        e = jnp.where(lm, lower, 0.0)
        tinv = tinv - _bmm(tinv, _bmm(e, tinv))

    uw = _bmm(tinv, jnp.stack(rhs_l))
    u, w = uw[:, :, :GDN_DK], uw[:, :, GDN_DK:]
    qg = jnp.stack([q_l[h] * eg_e[:, h * GDN_DK:(h + 1) * GDN_DK] for h in range(GDN_HEADS)])
    dl = jnp.stack([dlast_e[:, h * GDN_DK:(h + 1) * GDN_DK] for h in range(GDN_HEADS)])

    state = s_ref[...]
    v_news, outs = [], []
    for cix in range(nck):
        rs = slice(cix * ck, (cix + 1) * ck)
        r = _bmm(jnp.concatenate([w[:, rs], qg[:, rs]], axis=1), state)
        v_new = u[:, rs] - r[:, :ck]
        v_news.append(v_new)
        pad = [jnp.zeros((GDN_HEADS, ck, GDN_DK), F32)] * (nck - 1 - cix)
        outs.append(r[:, ck:] + _bmm(attn[:, rs], jnp.concatenate(v_news + pad, axis=1)))
        kt_t = jnp.stack([kt_l[h][rs].T for h in range(GDN_HEADS)])
        state = state * dl[:, cix * ck:cix * ck + 1] + _bmm(kt_t, v_new)
    s_ref[...] = state
    o = jnp.concatenate(outs, axis=1)
    for h in range(GDN_HEADS):
        sl = slice(h * GDN_DK, (h + 1) * GDN_DK)
        o_ref[:, sl] = _rms(o[h]) * nw_ref[...] * _silu(z_ref[:, sl])


def _gdn(q, k, v, z, b_raw, a_raw, a_log, dt_bias, norm_w, e6, bsz, seqlen, tile=256):
    nt = seqlen // tile
    wv = GDN_HEADS * GDN_DK
    row = lambda w: pl.BlockSpec((tile, w), lambda b, i: (b * nt + i, 0))
    return pl.pallas_call(
        functools.partial(_gdn_body, tile=tile),
        grid=(bsz, nt),
        in_specs=[row(wv), row(wv), row(wv), row(wv), row(LANES), row(LANES), _full((1, LANES)), _full((1, LANES)),
                  _full((1, GDN_DK)), _full(e6.shape)],
        out_specs=row(wv),
        out_shape=jax.ShapeDtypeStruct((bsz * seqlen, wv), F32),
        scratch_shapes=[pltpu.VMEM((GDN_HEADS, GDN_DK, GDN_DK), F32)],
        compiler_params=_params(2),
        name="gdn",
    )(q, k, v, z, b_raw, a_raw, a_log, dt_bias, norm_w, e6)


def _cmul_add(ar, ai, br, bi, cr, ci):
    return ar * br - ai * bi + cr, ar * bi + ai * br + ci


def _s5_body(u_ref, are_ref, aim_ref, ls_ref, bre_ref, bim_ref, cre_ref, cim_ref, d_ref, wg_ref, bg_ref,
             o_ref, pre_ref, pim_ref, qre_ref, qim_ref, h_ref, hre_ref, him_ref, *, tile):
    tt = tile
    sub = 8
    seg = tt // sub
    seg_bits = seg.bit_length() - 1
    ns = S5_GROUPS * S5_STATE
    b = pl.program_id(0)
    i = pl.program_id(1)

    @pl.when((b == 0) & (i == 0))
    def _():
        step = jnp.exp(ls_ref[...])

        def powers(n, scale):
            tpos = (lax.broadcasted_iota(jnp.int32, (n, 1), 0).astype(F32) + 1.0) * scale
            mag = jnp.exp(tpos * (are_ref[...] * step))
            ang = tpos * (aim_ref[...] * step)
            return mag * jnp.cos(ang), mag * jnp.sin(ang)

        pre_ref[...], pim_ref[...] = powers(seg, 1.0)
        qre_ref[...], qim_ref[...] = powers(sub, float(seg))

    @pl.when(i == 0)
    def _():
        h_ref[...] = jnp.zeros_like(h_ref)

    row_i = lax.broadcasted_iota(jnp.int32, (tt, tt), 0)
    col_i = lax.broadcasted_iota(jnp.int32, (tt, tt), 1)
    perm = col_i == (row_i & (sub - 1)) * seg + (row_i >> 3)
    unperm = col_i == (row_i & (seg - 1)) * sub + (row_i >> seg_bits)
    u = _rows01(perm, u_ref[...])

    bu_re = _mm(u, bre_ref[...])
    bu_im = _mm(u, bim_ref[...])
    a_re, a_im = are_ref[...], aim_ref[...]
    n_re, n_im = pre_ref[0:1, :] - 1.0, pim_ref[0:1, :]
    den = a_re * a_re + a_im * a_im
    c_re = (n_re * a_re + n_im * a_im) / den
    c_im = (n_im * a_re - n_re * a_im) / den
    hre_ref[...] = c_re * bu_re - c_im * bu_im
    him_ref[...] = c_re * bu_im + c_im * bu_re

    sl_i = lax.broadcasted_iota(jnp.int32, (sub, LANES), 0)

    def down(x, d, fill):
        return jnp.where(sl_i >= d, pltpu.roll(x, d, 0), fill)

    for j in range(ns // LANES):
        ls = slice(j * LANES, (j + 1) * LANES)
        ar, ai = pre_ref[0:1, ls], pim_ref[0:1, ls]
        gr, gi = hre_ref[0:sub, ls], him_ref[0:sub, ls]
        for t in range(1, seg):
            rs = slice(t * sub, (t + 1) * sub)
            gr, gi = _cmul_add(ar, ai, gr, gi, hre_ref[rs, ls], him_ref[rs, ls])
            hre_ref[rs, ls] = gr
            him_ref[rs, ls] = gi
        fr, fi = gr, gi
        for k in range(3):
            d = 1 << k
            mr, mi = qre_ref[d - 1:d, ls], qim_ref[d - 1:d, ls]
            fr, fi = _cmul_add(mr, mi, down(fr, d, 0.0), down(fi, d, 0.0), fr, fi)
        cr, ci = h_ref[0:1, ls], h_ref[1:2, ls]
        zr, zi = _cmul_add(qre_ref[:, ls], qim_ref[:, ls], cr, ci, fr, fi)
        h_ref[0:1, ls] = zr[sub - 1:sub, :]
        h_ref[1:2, ls] = zi[sub - 1:sub, :]
        er, ei = down(zr, 1, cr), down(zi, 1, ci)
        for t in range(seg):
            rs = slice(t * sub, (t + 1) * sub)
            hr, hi = _cmul_add(pre_ref[t:t + 1, ls], pim_ref[t:t + 1, ls], er, ei, hre_ref[rs, ls], him_ref[rs, ls])
            hre_ref[rs, ls] = hr
            him_ref[rs, ls] = hi

    y = _mm(hre_ref[...], cre_ref[...]) - _mm(him_ref[...], cim_ref[...]) + d_ref[...] * u
    y = jax.nn.gelu(y)
    o_ref[...] = _rows01(unperm, y * _sigmoid(_mm(y, wg_ref[...]) + bg_ref[...]))


def _s5(u, a_re, a_im, log_step, b_re, b_im, c_re, c_im, d_skip, w_glu, b_glu, bsz, seqlen, tile=256):
    nt = seqlen // tile
    ns = S5_GROUPS * S5_STATE
    row = pl.BlockSpec((tile, S5_CH), lambda b, i: (b * nt + i, 0))
    vec = _full((1, ns))
    return pl.pallas_call(
        functools.partial(_s5_body, tile=tile),
        grid=(bsz, nt),
        in_specs=[row, vec, vec, vec, _full((S5_CH, ns)), _full((S5_CH, ns)), _full((ns, S5_CH)),
                  _full((ns, S5_CH)), _full((1, S5_CH)), _full((S5_CH, S5_CH)), _full((1, S5_CH))],
        out_specs=row,
        out_shape=jax.ShapeDtypeStruct((bsz * seqlen, S5_CH), F32),
        scratch_shapes=[pltpu.VMEM((tile // 8, ns), F32), pltpu.VMEM((tile // 8, ns), F32),
                        pltpu.VMEM((8, ns), F32), pltpu.VMEM((8, ns), F32), pltpu.VMEM((8, ns), F32),
                        pltpu.VMEM((tile, ns), F32), pltpu.VMEM((tile, ns), F32)],
        compiler_params=_params(2),
        name="s5",
    )(u, a_re, a_im, log_step, b_re, b_im, c_re, c_im, d_skip, w_glu, b_glu)


def _pad_lanes(v, width=LANES):
    v = v.reshape(1, -1).astype(F32)
    return jnp.pad(v, ((0, 0), (0, width - v.shape[1])))


def _pad_cols(w, width=LANES):
    return jnp.pad(w, ((0, 0), (0, width - w.shape[1])))


def _head_expander(n_heads, head_width, rows=LANES):
    r = jnp.arange(rows)[:, None]
    col_head = jnp.arange(n_heads * head_width)[None, :] // head_width
    return (r == col_head).astype(BF16)


def _block_diag(blocks):
    g, r, c = blocks.shape
    eye = jnp.eye(g, dtype=blocks.dtype)
    return (eye[:, None, :, None] * blocks[:, :, None, :]).reshape(g * r, g * c)


def kernel(x, l0_norm_mix, l0_w_in, ssd_conv_w, ssd_conv_b, ssd_dt_bias, ssd_A_log, ssd_D, ssd_norm_w, l0_w_out, l0_norm_mlp, l0_w_up, l0_w_down, l1_norm_mix, l1_w_in, gdn_conv_w, gdn_A_log, gdn_dt_bias, gdn_norm_w, s5_A_re, s5_A_im, s5_log_step, s5_B_re, s5_B_im, s5_C_re, s5_C_im, s5_D, s5_w_glu, s5_b_glu, l1_w_out, l1_norm_mlp, l1_w_up, l1_w_down, final_norm):
    bsz, seqlen, d = x.shape
    x2d = x.reshape(bsz * seqlen, d)
    row = lambda v: v.reshape(1, -1).astype(F32)
    bf = lambda w: w.astype(BF16)

    ret_w = RET_HEADS * RET_DK
    ssd_inner = SSD_HEADS * SSD_P
    ssd_xbc = ssd_inner + 2 * SSD_GROUPS * SSD_N
    offs, acc = [], 0
    for wdt in (ret_w, ret_w, ret_w, ret_w, ssd_inner, ssd_inner, ssd_xbc - ssd_inner, SSD_HEADS):
        offs.append((acc, acc + wdt))
        acc += wdt
    wq, wk, wv, wgate, wz, wxs, wbc, wdt_ = [l0_w_in[:, a:b] for a, b in offs]
    half = RET_DK // 2
    inv = ROPE_THETA ** (-jnp.arange(half, dtype=F32) / half)
    inv2 = jnp.concatenate([inv, inv]).reshape(1, RET_DK)
    sgn = jnp.concatenate([-jnp.ones((half,), F32), jnp.ones((half,), F32)]).reshape(1, RET_DK)
    xs, bc, q, k, v, gate, z, dt = _norm_proj(
        x2d, row(l0_norm_mix), [bf(w) for w in (wxs, wbc, wq, wk, wv, wgate, wz, _pad_cols(wdt_))],
        ("conv_bias", "conv_bias", "rope_q", "rope_k", "plain", "plain", "plain", "plain"), seqlen,
        conv_w=ssd_conv_w.astype(F32), conv_b=row(ssd_conv_b), rope=(inv2, sgn))

    ret_out = _retention(q, k, v, gate, bsz, seqlen)
    ssd_out = _ssd(z, xs, bc, dt, _pad_lanes(ssd_dt_bias), _pad_lanes(ssd_A_log),
                   row(jnp.repeat(ssd_D.astype(F32), SSD_P)), row(ssd_norm_w),
                   _head_expander(SSD_HEADS, SSD_P), bsz, seqlen)

    x2d = _out_mlp(x2d, [ret_out, ssd_out], [bf(l0_w_out[:ret_w]), bf(l0_w_out[ret_w:])],
                   row(l0_norm_mlp), bf(l0_w_up), bf(l0_w_down))

    gdn_w = GDN_HEADS * GDN_DK
    offs, acc = [], 0
    for wdt in (gdn_w, gdn_w, gdn_w, gdn_w, GDN_HEADS, GDN_HEADS, S5_CH):
        offs.append((acc, acc + wdt))
        acc += wdt
    w1 = [l1_w_in[:, a:b] for a, b in offs]
    w1[4] = _pad_cols(w1[4])
    w1[5] = _pad_cols(w1[5])
    qg, kg, vg, zg, b_raw, a_raw, u = _norm_proj(
        x2d, row(l1_norm_mix), [bf(w) for w in w1],
        ("conv", "conv", "conv", "plain", "plain", "plain", "plain"), seqlen, conv_w=gdn_conv_w.astype(F32))

    gdn_out = _gdn(qg, kg, vg, zg, b_raw, a_raw, _pad_lanes(gdn_A_log), _pad_lanes(gdn_dt_bias), row(gdn_norm_w),
                   _head_expander(GDN_HEADS, GDN_DK), bsz, seqlen)

    flat = lambda p: p.reshape(1, S5_GROUPS * S5_STATE).astype(F32)
    b_re = _block_diag(jnp.swapaxes(s5_B_re.astype(F32), 1, 2))
    b_im = _block_diag(jnp.swapaxes(s5_B_im.astype(F32), 1, 2))
    c_re = _block_diag(jnp.swapaxes(s5_C_re.astype(F32), 1, 2))
    c_im = _block_diag(jnp.swapaxes(s5_C_im.astype(F32), 1, 2))
    s5_out = _s5(u, flat(s5_A_re), flat(s5_A_im), row(jnp.repeat(s5_log_step.astype(F32), S5_STATE)),
                 bf(b_re), bf(b_im), bf(c_re), bf(c_im), row(s5_D), bf(s5_w_glu), row(s5_b_glu), bsz, seqlen)

    out = _out_mlp(x2d, [gdn_out, s5_out], [bf(l1_w_out[:gdn_w]), bf(l1_w_out[gdn_w:])],
                   row(l1_norm_mlp), bf(l1_w_up), bf(l1_w_down), final_gain=row(final_norm))
    return out.reshape(bsz, seqlen, d)
```

```python
import functools
import math

import jax
import jax.numpy as jnp
from jax import lax
from jax.experimental import pallas as pl
from jax.experimental.pallas import tpu as pltpu

F32 = jnp.float32
BF16 = jnp.bfloat16
EPS = 1e-6
CONV_WIDTH = 4
HALO = 8

RET_HEADS, RET_DK = 4, 128
SSD_HEADS, SSD_P, SSD_GROUPS, SSD_N = 8, 64, 2, 128
GDN_HEADS, GDN_DK = 6, 128
GDN_CHUNK = 64
S5_CH, S5_GROUPS, S5_GROUP, S5_STATE = 256, 16, 16, 64
ROPE_THETA = 10000.0
LANES = 128

VMEM_LIMIT = 56 * 1024 * 1024


def _full(shape):
    nd = len(shape)
    return pl.BlockSpec(shape, lambda *_: (0,) * nd)


def _resident(shape):
    nd = len(shape)
    return pl.BlockSpec(shape, lambda *_: (0,) * nd, pipeline_mode=pl.Buffered(1))


def _params(n_axes):
    return pltpu.CompilerParams(dimension_semantics=("arbitrary",) * n_axes, vmem_limit_bytes=VMEM_LIMIT)


def _mm(a, b):
    return jnp.dot(a.astype(BF16), b.astype(BF16), preferred_element_type=F32)


def _mm_nt(a, b):
    return lax.dot_general(a.astype(BF16), b.astype(BF16), (((1,), (1,)), ((), ())), preferred_element_type=F32)


def _bmm(a, b):
    return lax.dot_general(a.astype(BF16), b.astype(BF16), (((2,), (1,)), ((0,), (0,))), preferred_element_type=F32)


def _bmm_nt(a, b):
    return lax.dot_general(a.astype(BF16), b.astype(BF16), (((2,), (2,)), ((0,), (0,))), preferred_element_type=F32)


def _split3(a):
    hi = a.astype(BF16)
    r1 = a - hi.astype(F32)
    mid = r1.astype(BF16)
    lo = (r1 - mid.astype(F32)).astype(BF16)
    return hi, mid, lo


def _rows01(m01, a):
    n = a.shape[1]
    p = jnp.dot(m01.astype(BF16), jnp.concatenate(_split3(a), axis=1), preferred_element_type=F32)
    return p[:, :n] + p[:, n:2 * n] + p[:, 2 * n:]


def _cols01(a, e01):
    hi, mid, lo = _split3(a)
    dot = lambda x: jnp.dot(x, e01, preferred_element_type=F32)
    return dot(hi) + dot(mid) + dot(lo)


def _sigmoid(x):
    return 1.0 / (1.0 + jnp.exp(-x))


def _silu(x):
    return x * _sigmoid(x)


def _softplus(x):
    return jnp.maximum(x, 0.0) + jnp.log(1.0 + jnp.exp(-jnp.abs(x)))


def _rms(x):
    return x * lax.rsqrt(jnp.mean(x * x, axis=-1, keepdims=True) + EPS)


def _norm_proj_body(*refs, kinds, tiles_per_seq):
    it = iter(refs)
    x_ref, g_ref = next(it), next(it)
    w_refs = [next(it) for _ in kinds]
    has_conv = any(k.startswith("conv") for k in kinds)
    has_rope = any(k.startswith("rope") for k in kinds)
    cw_ref = next(it) if has_conv else None
    cb_ref = next(it) if "conv_bias" in kinds else None
    inv_ref, sgn_ref = (next(it), next(it)) if has_rope else (None, None)
    o_refs = [next(it) for _ in kinds]
    xe_ref = next(it) if has_conv else None
    rc_ref, rs_ref = (next(it), next(it)) if has_rope else (None, None)

    tm = x_ref.shape[0]
    t_in_seq = pl.program_id(0) % tiles_per_seq

    if has_conv:
        @pl.when(t_in_seq == 0)
        def _():
            xe_ref[:, 0:HALO, :] = jnp.zeros((xe_ref.shape[0], HALO, LANES), F32)

    if has_rope:
        @pl.when(pl.program_id(0) == 0)
        def _():
            ang_r = lax.broadcasted_iota(jnp.int32, (tm, 1), 0).astype(F32) * inv_ref[...]
            rc_ref[...] = jnp.cos(ang_r)
            rs_ref[...] = jnp.sin(ang_r)

        ang_0 = (t_in_seq * tm).astype(F32) * inv_ref[...]
        c0, s0 = jnp.cos(ang_0), jnp.sin(ang_0)
        cos = rc_ref[...] * c0 - rs_ref[...] * s0
        sin = (rs_ref[...] * c0 + rc_ref[...] * s0) * sgn_ref[...]

    xb = (_rms(x_ref[...]) * g_ref[...]).astype(BF16)
    conv_col = 0
    for kind, w_ref, o_ref in zip(kinds, w_refs, o_refs):
        y = jnp.dot(xb, w_ref[...], preferred_element_type=F32)
        if kind.startswith("rope"):
            scale = RET_DK ** -0.5 if kind == "rope_q" else 1.0
            for h in range(y.shape[1] // RET_DK):
                sl = slice(h * RET_DK, (h + 1) * RET_DK)
                yh = y[:, sl]
                o_ref[:, sl] = ((yh * cos + pltpu.roll(yh, RET_DK // 2, 1) * sin) * scale).astype(o_ref.dtype)
        elif kind.startswith("conv"):
            for lt in range(y.shape[1] // LANES):
                ct = conv_col // LANES + lt
                ls = slice(conv_col + lt * LANES, conv_col + (lt + 1) * LANES)
                xe_ref[ct, HALO:HALO + tm, :] = y[:, lt * LANES:(lt + 1) * LANES]
                acc = None
                for j in range(CONV_WIDTH):
                    off = HALO - (CONV_WIDTH - 1) + j
                    term = cw_ref[j:j + 1, ls] * xe_ref[ct, off:off + tm, :]
                    acc = term if acc is None else acc + term
                if kind == "conv_bias":
                    acc = acc + cb_ref[:, ls]
                o_ref[:, lt * LANES:(lt + 1) * LANES] = _silu(acc).astype(o_ref.dtype)
                xe_ref[ct, 0:HALO, :] = xe_ref[ct, tm:tm + HALO, :]
            conv_col += y.shape[1]
        else:
            o_ref[...] = y.astype(o_ref.dtype)


def _norm_proj(x2d, gain, weights, kinds, seqlen, conv_w=None, conv_b=None, rope=None, out_dtypes=None, tm=512):
    t, d = x2d.shape
    out_dtypes = out_dtypes or (F32,) * len(weights)
    in_specs = [pl.BlockSpec((tm, d), lambda i: (i, 0)), _full((1, d))] + [_resident(w.shape) for w in weights]
    args = [x2d, gain, *weights]
    scratch = []
    if conv_w is not None:
        in_specs.append(_full(conv_w.shape))
        args.append(conv_w)
        if conv_b is not None:
            in_specs.append(_full(conv_b.shape))
            args.append(conv_b)
        scratch.append(pltpu.VMEM((conv_w.shape[1] // LANES, tm + HALO, LANES), F32))
    if rope is not None:
        in_specs += [_full(r.shape) for r in rope]
        args += list(rope)
        scratch += [pltpu.VMEM((tm, RET_DK), F32), pltpu.VMEM((tm, RET_DK), F32)]
    return pl.pallas_call(
        functools.partial(_norm_proj_body, kinds=tuple(kinds), tiles_per_seq=seqlen // tm),
        grid=(t // tm,),
        in_specs=in_specs,
        out_specs=[pl.BlockSpec((tm, w.shape[1]), lambda i: (i, 0)) for w in weights],
        out_shape=[jax.ShapeDtypeStruct((t, w.shape[1]), dt) for w, dt in zip(weights, out_dtypes)],
        scratch_shapes=scratch,
        compiler_params=_params(1),
        name="norm_proj",
    )(*args)


def _out_mlp_body(x_ref, *refs, n_mix, ff_chunk, final):
    mix_refs, wo_refs = refs[:n_mix], refs[n_mix:2 * n_mix]
    rest = refs[2 * n_mix:]
    if final:
        g_ref, wu_ref, wd_ref, gf_ref, o_ref = rest
    else:
        g_ref, wu_ref, wd_ref, o_ref = rest
    d_ff = wu_ref.shape[1]
    x1 = x_ref[...]
    for m_ref, w_ref in zip(mix_refs, wo_refs):
        x1 = x1 + jnp.dot(m_ref[...].astype(BF16), w_ref[...], preferred_element_type=F32)
    xb = (_rms(x1) * g_ref[...]).astype(BF16)
    mlp = None
    for c in range(d_ff // ff_chunk):
        a = jnp.dot(xb, wu_ref[:, c * ff_chunk:(c + 1) * ff_chunk], preferred_element_type=F32)
        a = jnp.maximum(a, 0.0)
        t = jnp.dot((a * a).astype(BF16), wd_ref[c * ff_chunk:(c + 1) * ff_chunk, :],
                    preferred_element_type=F32)
        mlp = t if mlp is None else mlp + t
    acc = x1 + mlp
    if final:
        acc = _rms(acc) * gf_ref[...]
    o_ref[...] = acc


def _out_mlp(x2d, mixes, w_outs, gain, w_up, w_down, final_gain=None, tm=512, ff_chunk=1024):
    t, d = x2d.shape
    n_mix = len(mixes)
    final = final_gain is not None
    row = lambda w: pl.BlockSpec((tm, w), lambda i: (i, 0))
    in_specs = [row(d)] + [row(m.shape[1]) for m in mixes] + [_resident(w.shape) for w in w_outs]
    in_specs += [_full((1, d)), _resident(w_up.shape), _resident(w_down.shape)]
    args = [x2d, *mixes, *w_outs, gain, w_up, w_down]
    if final:
        in_specs.append(_full((1, d)))
        args.append(final_gain)
    return pl.pallas_call(
        functools.partial(_out_mlp_body, n_mix=n_mix, ff_chunk=ff_chunk, final=final),
        grid=(t // tm,),
        in_specs=in_specs,
        out_specs=row(d),
        out_shape=jax.ShapeDtypeStruct((t, d), F32),
        compiler_params=_params(1),
        name="out_mlp",
    )(*args)


def _retention_body(q_ref, k_ref, v_ref, g_ref, o_ref, s_ref, dm_ref, *, chunk):
    c = chunk
    i = pl.program_id(1)
    gammas = [math.log(1.0 - 2.0 ** (-5.0 - h)) for h in range(RET_HEADS)]

    @pl.when(i == 0)
    def _():
        s_ref[...] = jnp.zeros_like(s_ref)
        rowi = lax.broadcasted_iota(jnp.int32, (c, c), 0)
        coli = lax.broadcasted_iota(jnp.int32, (c, c), 1)
        diff = (rowi - coli).astype(F32)
        for h in range(RET_HEADS):
            dm_ref[h] = jnp.where(rowi >= coli, jnp.exp(gammas[h] * diff), 0.0)

    idx = lax.broadcasted_iota(jnp.int32, (c, 1), 0).astype(F32)
    for h in range(RET_HEADS):
        sl = slice(h * RET_DK, (h + 1) * RET_DK)
        lg = gammas[h]
        qh = q_ref[:, sl]
        kh = k_ref[:, sl]
        vh = v_ref[:, sl]
        scores = _mm_nt(qh, kh) * dm_ref[h]
        state = s_ref[h]
        y = _mm(scores, vh) + _mm(qh * jnp.exp(lg * (idx + 1.0)), state)
        kw = kh * jnp.exp(lg * (c - 1.0 - idx))
        s_ref[h] = state * math.exp(lg * c) + _mm(kw.T, vh)
        gt = g_ref[:, sl]
        o_ref[:, sl] = (_silu(gt) * _rms(y)).astype(o_ref.dtype)


def _retention(q, k, v, gate, bsz, seqlen, chunk=256):
    nt = seqlen // chunk
    w = RET_HEADS * RET_DK
    row = pl.BlockSpec((chunk, w), lambda b, i: (b * nt + i, 0))
    return pl.pallas_call(
        functools.partial(_retention_body, chunk=chunk),
        grid=(bsz, nt),
        in_specs=[row, row, row, row],
        out_specs=row,
        out_shape=jax.ShapeDtypeStruct((bsz * seqlen, w), BF16),
        scratch_shapes=[pltpu.VMEM((RET_HEADS, RET_DK, RET_DK), F32),
                        pltpu.VMEM((RET_HEADS, chunk, chunk), F32)],
        compiler_params=_params(2),
        name="retention",
    )(q, k, v, gate)


def _ssd_body(z_ref, xs_ref, bc_ref, dt_ref, dtb_ref, al_ref, de_ref, nw_ref, e8_ref, o_ref, s_ref, *, chunk):
    c = chunk
    i = pl.program_id(1)
    inner = SSD_HEADS * SSD_P
    gw = inner // SSD_GROUPS
    hpg = SSD_HEADS // SSD_GROUPS

    @pl.when(i == 0)
    def _():
        s_ref[...] = jnp.zeros_like(s_ref)

    xs = xs_ref[...]
    bm = bc_ref[:, :SSD_GROUPS * SSD_N]
    cm = bc_ref[:, SSD_GROUPS * SSD_N:]

    dtv = _softplus(dt_ref[...] + dtb_ref[...])
    la = dtv * (-jnp.exp(al_ref[...]))
    rowi = lax.broadcasted_iota(jnp.int32, (c, c), 0)
    coli = lax.broadcasted_iota(jnp.int32, (c, c), 1)
    causal = rowi >= coli
    cum = _rows01(causal, la)
    cum_t = cum.T
    e8 = e8_ref[...]
    dt_e = _cols01(dtv, e8)
    cum_e = _cols01(cum, e8)
    last_e = cum_e[c - 1:c, :]
    xd = xs * dt_e
    xdw = xd * jnp.exp(last_e - cum_e)
    lane = lax.broadcasted_iota(jnp.int32, (1, gw), 1)

    ys = []
    for g in range(SSD_GROUPS):
        bg = bm[:, g * SSD_N:(g + 1) * SSD_N]
        cg = cm[:, g * SSD_N:(g + 1) * SSD_N]
        gsl = slice(g * gw, (g + 1) * gw)
        cb = _mm_nt(cg, bg)
        xd_g = xd[:, gsl]
        yd = None
        for j in range(hpg):
            h = g * hpg + j
            seg = cum[:, h:h + 1] - cum_t[h:h + 1, :]
            lmat = jnp.exp(jnp.where(causal, seg, -jnp.inf))
            xm = jnp.where((lane >= SSD_P * j) & (lane < SSD_P * (j + 1)), xd_g, 0.0)
            t = _mm(cb * lmat, xm)
            yd = t if yd is None else yd + t
        state = s_ref[g]
        y_off = _mm(cg, state) * jnp.exp(cum_e[:, gsl])
        s_ref[g] = state * jnp.exp(last_e[:, gsl]) + _mm(bg.astype(F32).T, xdw[:, gsl])
        ys.append(yd + y_off)

    zz = z_ref[...]
    for g in range(SSD_GROUPS):
        gsl = slice(g * gw, (g + 1) * gw)
        yg = (ys[g] + de_ref[:, gsl] * xs[:, gsl]) * _silu(zz[:, gsl])
        o_ref[:, gsl] = (_rms(yg) * nw_ref[:, gsl]).astype(o_ref.dtype)


def _ssd(z, xs, bc, dt, dt_bias, a_log, d_e, norm_w, e8, bsz, seqlen, chunk=256):
    nt = seqlen // chunk
    inner = SSD_HEADS * SSD_P
    row = lambda w: pl.BlockSpec((chunk, w), lambda b, i: (b * nt + i, 0))
    return pl.pallas_call(
        functools.partial(_ssd_body, chunk=chunk),
        grid=(bsz, nt),
        in_specs=[row(inner), row(inner), row(bc.shape[1]), row(LANES), _full((1, LANES)), _full((1, LANES)),
                  _full((1, inner)), _full((1, inner)), _full(e8.shape)],
        out_specs=row(inner),
        out_shape=jax.ShapeDtypeStruct((bsz * seqlen, inner), BF16),
        scratch_shapes=[pltpu.VMEM((SSD_GROUPS, SSD_N, inner // SSD_GROUPS), F32)],
        compiler_params=_params(2),
        name="ssd",
    )(z, xs, bc, dt, dt_bias, a_log, d_e, norm_w, e8)


def _gdn_body(q_ref, k_ref, v_ref, z_ref, b_ref, a_ref, al_ref, dtb_ref, nw_ref, e6_ref, o_ref, s_ref, *, tile):
    bsz = q_ref.shape[0]
    ck = GDN_CHUNK
    nck = tile // ck
    nall = bsz * nck
    i = pl.program_id(0)
    qk_w = GDN_HEADS * GDN_DK
    rows_of = lambda ref: jnp.concatenate([ref[b] for b in range(bsz)], axis=0)

    @pl.when(i == 0)
    def _():
        s_ref[...] = jnp.zeros_like(s_ref)

    beta = _sigmoid(rows_of(b_ref))
    gl = -jnp.exp(al_ref[...]) * _softplus(rows_of(a_ref) + dtb_ref[...])

    rowt = lax.broadcasted_iota(jnp.int32, (tile, tile), 0)
    colt = lax.broadcasted_iota(jnp.int32, (tile, tile), 1)
    chunk_causal = ((rowt >> 6) == (colt >> 6)) & (rowt >= colt)
    gc = jnp.concatenate([_rows01(chunk_causal, gl[b * tile:(b + 1) * tile]) for b in range(bsz)], axis=0)
    gc_t = gc.T
    e6 = e6_ref[...]
    gc_e = _cols01(gc, e6)
    beta_e = _cols01(beta, e6)
    eg_e = jnp.exp(gc_e)
    glast_rows = [jnp.broadcast_to(gc_e[(cix + 1) * ck - 1:(cix + 1) * ck, :], (ck, qk_w)) for cix in range(nall)]
    glast_e = jnp.concatenate(glast_rows, axis=0)
    tail_e = jnp.exp(glast_e - gc_e)
    dlast_e = jnp.exp(glast_e)

    rowi = lax.broadcasted_iota(jnp.int32, (ck, ck), 0)
    coli = lax.broadcasted_iota(jnp.int32, (ck, ck), 1)
    causal = rowi >= coli
    strict = rowi > coli

    def same_block(bits):
        return (rowi >> bits) == (coli >> bits)

    eye = (rowi == coli).astype(F32)
    base_mask = same_block(3) & strict
    level_masks = [same_block(b + 1) & (~same_block(b)) & strict for b in (3, 4, 5)]

    q_all, k_all, v_all = rows_of(q_ref), rows_of(k_ref), rows_of(v_ref)
    q_l, k_l, kb_l, rhs_l, kt_l, qg_l, decay_l = [], [], [], [], [], [], []
    for h in range(GDN_HEADS):
        sl = slice(h * GDN_DK, (h + 1) * GDN_DK)
        qh = q_all[:, sl]
        kh = k_all[:, sl]
        vh = v_all[:, sl]
        qh = qh * lax.rsqrt(jnp.sum(qh * qh, axis=-1, keepdims=True) + EPS) * (GDN_DK ** -0.5)
        kh = kh * lax.rsqrt(jnp.sum(kh * kh, axis=-1, keepdims=True) + EPS)
        bh = beta_e[:, sl]
        kb = kh * bh
        rhs = jnp.concatenate([vh * bh, kb * eg_e[:, sl]], axis=1)
        kt = kh * tail_e[:, sl]
        qg = qh * eg_e[:, sl]
        for cix in range(nall):
            rs = slice(cix * ck, (cix + 1) * ck)
            q_l.append(qh[rs])
            k_l.append(kh[rs])
            kb_l.append(kb[rs])
            rhs_l.append(rhs[rs])
            kt_l.append(kt[rs])
            qg_l.append(qg[rs])
            seg = gc[rs, h:h + 1] - gc_t[h:h + 1, rs]
            decay_l.append(jnp.exp(jnp.where(causal, seg, -jnp.inf)))
    q3, k3, kb3 = jnp.stack(q_l), jnp.stack(k_l), jnp.stack(kb_l)
    decay = jnp.stack(decay_l)
    lower = jnp.where(strict, _bmm_nt(kb3, k3) * decay, 0.0)
    attn = _bmm_nt(q3, k3) * decay

    d8 = jnp.where(base_mask, lower, 0.0)
    x2 = _bmm(d8, d8)
    x4 = _bmm(x2, x2)
    tinv = eye - d8
    tinv = tinv + _bmm(tinv, x2)
    tinv = tinv + _bmm(tinv, x4)
    for lm in level_masks:
        e = jnp.where(lm, lower, 0.0)
        tinv = tinv - _bmm(tinv, _bmm(e, tinv))

    uw = _bmm(tinv, jnp.stack(rhs_l))

    def of_chunk(xs, cix):
        return jnp.stack([xs[h * nall + b * nck + cix] for b in range(bsz) for h in range(GDN_HEADS)])

    state = s_ref[...]
    outs = []
    for cix in range(nck):
        uw_c = of_chunk(uw, cix)
        r = _bmm(jnp.concatenate([uw_c[:, :, GDN_DK:], of_chunk(qg_l, cix)], axis=1), state)
        v_new = uw_c[:, :, :GDN_DK] - r[:, :ck]
        outs.append(r[:, ck:] + _bmm(of_chunk(attn, cix), v_new))
        kt_t = jnp.stack([kt_l[h * nall + b * nck + cix].T for b in range(bsz) for h in range(GDN_HEADS)])
        dl = jnp.stack([dlast_e[(b * nck + cix) * ck:(b * nck + cix) * ck + 1, h * GDN_DK:(h + 1) * GDN_DK]
                        for b in range(bsz) for h in range(GDN_HEADS)])
        state = state * dl + _bmm(kt_t, v_new)
    s_ref[...] = state
    for b in range(bsz):
        for h in range(GDN_HEADS):
            sl = slice(h * GDN_DK, (h + 1) * GDN_DK)
            o = jnp.concatenate([outs[cix][b * GDN_HEADS + h] for cix in range(nck)], axis=0)
            o_ref[b, :, sl] = (_rms(o) * nw_ref[...] * _silu(z_ref[b, :, sl])).astype(o_ref.dtype)


def _gdn(q, k, v, z, b_raw, a_raw, a_log, dt_bias, norm_w, e6, bsz, seqlen, tile=256):
    nt = seqlen // tile
    wv = GDN_HEADS * GDN_DK
    row = lambda w: pl.BlockSpec((bsz, tile, w), lambda i: (0, i, 0))
    seq = lambda a: a.reshape(bsz, seqlen, a.shape[-1])
    return pl.pallas_call(
        functools.partial(_gdn_body, tile=tile),
        grid=(nt,),
        in_specs=[row(wv), row(wv), row(wv), row(wv), row(LANES), row(LANES), _full((1, LANES)), _full((1, LANES)),
                  _full((1, GDN_DK)), _full(e6.shape)],
        out_specs=row(wv),
        out_shape=jax.ShapeDtypeStruct((bsz, seqlen, wv), BF16),
        scratch_shapes=[pltpu.VMEM((bsz * GDN_HEADS, GDN_DK, GDN_DK), F32)],
        compiler_params=_params(1),
        name="gdn",
    )(seq(q), seq(k), seq(v), seq(z), seq(b_raw), seq(a_raw), a_log, dt_bias, norm_w, e6).reshape(bsz * seqlen, wv)


def _cmul_add(ar, ai, br, bi, cr, ci):
    return ar * br - ai * bi + cr, ar * bi + ai * br + ci


def _s5_body(u_ref, are_ref, aim_ref, ls_ref, bre_ref, bim_ref, cre_ref, cim_ref, d_ref, wg_ref, bg_ref,
             o_ref, pre_ref, pim_ref, qre_ref, qim_ref, h_ref, hre_ref, him_ref, *, tile):
    tt = tile
    sub = 8
    seg = tt // sub
    seg_bits = seg.bit_length() - 1
    ns = S5_GROUPS * S5_STATE
    b = pl.program_id(0)
    i = pl.program_id(1)

    @pl.when((b == 0) & (i == 0))
    def _():
        step = jnp.exp(ls_ref[...])

        def powers(n, scale):
            tpos = (lax.broadcasted_iota(jnp.int32, (n, 1), 0).astype(F32) + 1.0) * scale
            mag = jnp.exp(tpos * (are_ref[...] * step))
            ang = tpos * (aim_ref[...] * step)
            return mag * jnp.cos(ang), mag * jnp.sin(ang)

        pre_ref[...], pim_ref[...] = powers(seg, 1.0)
        qre_ref[...], qim_ref[...] = powers(sub, float(seg))

    @pl.when(i == 0)
    def _():
        h_ref[...] = jnp.zeros_like(h_ref)

    row_i = lax.broadcasted_iota(jnp.int32, (tt, tt), 0)
    col_i = lax.broadcasted_iota(jnp.int32, (tt, tt), 1)
    perm = col_i == (row_i & (sub - 1)) * seg + (row_i >> 3)
    unperm = col_i == (row_i & (seg - 1)) * sub + (row_i >> seg_bits)
    u = _rows01(perm, u_ref[...])

    bu_re = _mm(u, bre_ref[...])
    bu_im = _mm(u, bim_ref[...])
    a_re, a_im = are_ref[...], aim_ref[...]
    n_re, n_im = pre_ref[0:1, :] - 1.0, pim_ref[0:1, :]
    den = a_re * a_re + a_im * a_im
    c_re = (n_re * a_re + n_im * a_im) / den
    c_im = (n_im * a_re - n_re * a_im) / den
    hre_ref[...] = c_re * bu_re - c_im * bu_im
    him_ref[...] = c_re * bu_im + c_im * bu_re

    sl_i = lax.broadcasted_iota(jnp.int32, (sub, LANES), 0)

    def down(x, d, fill):
        return jnp.where(sl_i >= d, pltpu.roll(x, d, 0), fill)

    for j in range(ns // LANES):
        ls = slice(j * LANES, (j + 1) * LANES)
        ar, ai = pre_ref[0:1, ls], pim_ref[0:1, ls]
        gr, gi = hre_ref[0:sub, ls], him_ref[0:sub, ls]
        for t in range(1, seg):
            rs = slice(t * sub, (t + 1) * sub)
            gr, gi = _cmul_add(ar, ai, gr, gi, hre_ref[rs, ls], him_ref[rs, ls])
            hre_ref[rs, ls] = gr
            him_ref[rs, ls] = gi
        fr, fi = gr, gi
        for k in range(3):
            d = 1 << k
            mr, mi = qre_ref[d - 1:d, ls], qim_ref[d - 1:d, ls]
            fr, fi = _cmul_add(mr, mi, down(fr, d, 0.0), down(fi, d, 0.0), fr, fi)
        cr, ci = h_ref[0:1, ls], h_ref[1:2, ls]
        zr, zi = _cmul_add(qre_ref[:, ls], qim_ref[:, ls], cr, ci, fr, fi)
        h_ref[0:1, ls] = zr[sub - 1:sub, :]
        h_ref[1:2, ls] = zi[sub - 1:sub, :]
        er, ei = down(zr, 1, cr), down(zi, 1, ci)
        for t in range(seg):
            rs = slice(t * sub, (t + 1) * sub)
            hr, hi = _cmul_add(pre_ref[t:t + 1, ls], pim_ref[t:t + 1, ls], er, ei, hre_ref[rs, ls], him_ref[rs, ls])
            hre_ref[rs, ls] = hr
            him_ref[rs, ls] = hi

    y = _mm(hre_ref[...], cre_ref[...]) - _mm(him_ref[...], cim_ref[...]) + d_ref[...] * u
    y = jax.nn.gelu(y)
    o_ref[...] = _mm(unperm, y * _sigmoid(_mm(y, wg_ref[...]) + bg_ref[...])).astype(o_ref.dtype)


def _s5(u, a_re, a_im, log_step, b_re, b_im, c_re, c_im, d_skip, w_glu, b_glu, bsz, seqlen, tile=256):
    nt = seqlen // tile
    ns = S5_GROUPS * S5_STATE
    row = pl.BlockSpec((tile, S5_CH), lambda b, i: (b * nt + i, 0))
    vec = _full((1, ns))
    return pl.pallas_call(
        functools.partial(_s5_body, tile=tile),
        grid=(bsz, nt),
        in_specs=[row, vec, vec, vec, _full((S5_CH, ns)), _full((S5_CH, ns)), _full((ns, S5_CH)),
                  _full((ns, S5_CH)), _full((1, S5_CH)), _full((S5_CH, S5_CH)), _full((1, S5_CH))],
        out_specs=row,
        out_shape=jax.ShapeDtypeStruct((bsz * seqlen, S5_CH), BF16),
        scratch_shapes=[pltpu.VMEM((tile // 8, ns), F32), pltpu.VMEM((tile // 8, ns), F32),
                        pltpu.VMEM((8, ns), F32), pltpu.VMEM((8, ns), F32), pltpu.VMEM((8, ns), F32),
                        pltpu.VMEM((tile, ns), F32), pltpu.VMEM((tile, ns), F32)],
        compiler_params=_params(2),
        name="s5",
    )(u, a_re, a_im, log_step, b_re, b_im, c_re, c_im, d_skip, w_glu, b_glu)


def _pad_lanes(v, width=LANES):
    v = v.reshape(1, -1).astype(F32)
    return jnp.pad(v, ((0, 0), (0, width - v.shape[1])))


def _pad_cols(w, width=LANES):
    return jnp.pad(w, ((0, 0), (0, width - w.shape[1])))


def _head_expander(n_heads, head_width, rows=LANES):
    r = jnp.arange(rows)[:, None]
    col_head = jnp.arange(n_heads * head_width)[None, :] // head_width
    return (r == col_head).astype(BF16)


def _block_diag(blocks):
    g, r, c = blocks.shape
    eye = jnp.eye(g, dtype=blocks.dtype)
    return (eye[:, None, :, None] * blocks[:, :, None, :]).reshape(g * r, g * c)


def kernel(x, l0_norm_mix, l0_w_in, ssd_conv_w, ssd_conv_b, ssd_dt_bias, ssd_A_log, ssd_D, ssd_norm_w, l0_w_out, l0_norm_mlp, l0_w_up, l0_w_down, l1_norm_mix, l1_w_in, gdn_conv_w, gdn_A_log, gdn_dt_bias, gdn_norm_w, s5_A_re, s5_A_im, s5_log_step, s5_B_re, s5_B_im, s5_C_re, s5_C_im, s5_D, s5_w_glu, s5_b_glu, l1_w_out, l1_norm_mlp, l1_w_up, l1_w_down, final_norm):
    bsz, seqlen, d = x.shape
    x2d = x.reshape(bsz * seqlen, d)
    row = lambda v: v.reshape(1, -1).astype(F32)
    bf = lambda w: w.astype(BF16)

    ret_w = RET_HEADS * RET_DK
    ssd_inner = SSD_HEADS * SSD_P
    ssd_xbc = ssd_inner + 2 * SSD_GROUPS * SSD_N
    offs, acc = [], 0
    for wdt in (ret_w, ret_w, ret_w, ret_w, ssd_inner, ssd_inner, ssd_xbc - ssd_inner, SSD_HEADS):
        offs.append((acc, acc + wdt))
        acc += wdt
    wq, wk, wv, wgate, wz, wxs, wbc, wdt_ = [l0_w_in[:, a:b] for a, b in offs]
    half = RET_DK // 2
    inv = ROPE_THETA ** (-jnp.arange(half, dtype=F32) / half)
    inv2 = jnp.concatenate([inv, inv]).reshape(1, RET_DK)
    sgn = jnp.concatenate([-jnp.ones((half,), F32), jnp.ones((half,), F32)]).reshape(1, RET_DK)
    xs, bc, q, k, v, gate, z, dt = _norm_proj(
        x2d, row(l0_norm_mix), [bf(w) for w in (wxs, wbc, wq, wk, wv, wgate, wz, _pad_cols(wdt_))],
        ("conv_bias", "conv_bias", "rope_q", "rope_k", "plain", "plain", "plain", "plain"), seqlen,
        conv_w=ssd_conv_w.astype(F32), conv_b=row(ssd_conv_b), rope=(inv2, sgn),
        out_dtypes=(F32, BF16, BF16, BF16, BF16, F32, F32, F32))

    ret_out = _retention(q, k, v, gate, bsz, seqlen)
    ssd_out = _ssd(z, xs, bc, dt, _pad_lanes(ssd_dt_bias), _pad_lanes(ssd_A_log),
                   row(jnp.repeat(ssd_D.astype(F32), SSD_P)), row(ssd_norm_w),
                   _head_expander(SSD_HEADS, SSD_P), bsz, seqlen)

    x2d = _out_mlp(x2d, [ret_out, ssd_out], [bf(l0_w_out[:ret_w]), bf(l0_w_out[ret_w:])],
                   row(l0_norm_mlp), bf(l0_w_up), bf(l0_w_down))

    gdn_w = GDN_HEADS * GDN_DK
    offs, acc = [], 0
    for wdt in (gdn_w, gdn_w, gdn_w, gdn_w, GDN_HEADS, GDN_HEADS, S5_CH):
        offs.append((acc, acc + wdt))
        acc += wdt
    w1 = [l1_w_in[:, a:b] for a, b in offs]
    w1[4] = _pad_cols(w1[4])
    w1[5] = _pad_cols(w1[5])
    qg, kg, vg, zg, b_raw, a_raw, u = _norm_proj(
        x2d, row(l1_norm_mix), [bf(w) for w in w1],
        ("conv", "conv", "conv", "plain", "plain", "plain", "plain"), seqlen, conv_w=gdn_conv_w.astype(F32))

    gdn_out = _gdn(qg, kg, vg, zg, b_raw, a_raw, _pad_lanes(gdn_A_log), _pad_lanes(gdn_dt_bias), row(gdn_norm_w),
                   _head_expander(GDN_HEADS, GDN_DK), bsz, seqlen)

    flat = lambda p: p.reshape(1, S5_GROUPS * S5_STATE).astype(F32)
    b_re = _block_diag(jnp.swapaxes(s5_B_re.astype(F32), 1, 2))
    b_im = _block_diag(jnp.swapaxes(s5_B_im.astype(F32), 1, 2))
    c_re = _block_diag(jnp.swapaxes(s5_C_re.astype(F32), 1, 2))
    c_im = _block_diag(jnp.swapaxes(s5_C_im.astype(F32), 1, 2))
    s5_out = _s5(u, flat(s5_A_re), flat(s5_A_im), row(jnp.repeat(s5_log_step.astype(F32), S5_STATE)),
                 bf(b_re), bf(b_im), bf(c_re), bf(c_im), row(s5_D), bf(s5_w_glu), row(s5_b_glu), bsz, seqlen)

    out = _out_mlp(x2d, [gdn_out, s5_out], [bf(l1_w_out[:gdn_w]), bf(l1_w_out[gdn_w:])],
                   row(l1_norm_mlp), bf(l1_w_up), bf(l1_w_down), final_gain=row(final_norm))
    return out.reshape(bsz, seqlen, d)
```

```python
import functools
import math

import jax
import jax.numpy as jnp
from jax import lax
from jax.experimental import pallas as pl
from jax.experimental.pallas import tpu as pltpu

F32 = jnp.float32
BF16 = jnp.bfloat16
EPS = 1e-6
CONV_WIDTH = 4
HALO = 8

RET_HEADS, RET_DK = 4, 128
SSD_HEADS, SSD_P, SSD_GROUPS, SSD_N = 8, 64, 2, 128
GDN_HEADS, GDN_DK = 6, 128
GDN_CHUNK = 64
S5_CH, S5_GROUPS, S5_GROUP, S5_STATE = 256, 16, 16, 64
ROPE_THETA = 10000.0
LANES = 128

VMEM_LIMIT = 56 * 1024 * 1024


def _full(shape):
    nd = len(shape)
    return pl.BlockSpec(shape, lambda *_: (0,) * nd)


def _resident(shape):
    nd = len(shape)
    return pl.BlockSpec(shape, lambda *_: (0,) * nd, pipeline_mode=pl.Buffered(1))


def _params(n_axes):
    return pltpu.CompilerParams(dimension_semantics=("arbitrary",) * n_axes, vmem_limit_bytes=VMEM_LIMIT)


def _mm(a, b):
    return jnp.dot(a.astype(BF16), b.astype(BF16), preferred_element_type=F32)


def _mm_nt(a, b):
    return lax.dot_general(a.astype(BF16), b.astype(BF16), (((1,), (1,)), ((), ())), preferred_element_type=F32)


def _bmm(a, b):
    return lax.dot_general(a.astype(BF16), b.astype(BF16), (((2,), (1,)), ((0,), (0,))), preferred_element_type=F32)


def _bmm_nt(a, b):
    return lax.dot_general(a.astype(BF16), b.astype(BF16), (((2,), (2,)), ((0,), (0,))), preferred_element_type=F32)


def _split3(a):
    hi = a.astype(BF16)
    r1 = a - hi.astype(F32)
    mid = r1.astype(BF16)
    lo = (r1 - mid.astype(F32)).astype(BF16)
    return hi, mid, lo


def _rows01(m01, a):
    n = a.shape[1]
    p = jnp.dot(m01.astype(BF16), jnp.concatenate(_split3(a), axis=1), preferred_element_type=F32)
    return p[:, :n] + p[:, n:2 * n] + p[:, 2 * n:]


def _cols01(a, e01):
    hi, mid, lo = _split3(a)
    dot = lambda x: jnp.dot(x, e01, preferred_element_type=F32)
    return dot(hi) + dot(mid) + dot(lo)


def _sigmoid(x):
    return 1.0 / (1.0 + jnp.exp(-x))


def _silu(x):
    return x * _sigmoid(x)


def _softplus(x):
    return jnp.maximum(x, 0.0) + jnp.log(1.0 + jnp.exp(-jnp.abs(x)))


def _rms(x):
    return x * lax.rsqrt(jnp.mean(x * x, axis=-1, keepdims=True) + EPS)


def _norm_proj_body(*refs, kinds, tiles_per_seq, sub_rows):
    it = iter(refs)
    x_ref, g_ref = next(it), next(it)
    w_refs = [next(it) for _ in kinds]
    has_conv = any(k.startswith("conv") for k in kinds)
    has_rope = any(k.startswith("rope") for k in kinds)
    cw_ref = next(it) if has_conv else None
    cb_ref = next(it) if "conv_bias" in kinds else None
    inv_ref, sgn_ref = (next(it), next(it)) if has_rope else (None, None)
    o_refs = [next(it) for _ in kinds]
    xe_ref = next(it) if has_conv else None
    rc_ref, rs_ref = (next(it), next(it)) if has_rope else (None, None)

    tm = x_ref.shape[0]
    t_in_seq = pl.program_id(0) % tiles_per_seq

    if has_conv:
        @pl.when(t_in_seq == 0)
        def _():
            xe_ref[:, 0:HALO, :] = jnp.zeros((xe_ref.shape[0], HALO, LANES), F32)

    if has_rope:
        @pl.when(pl.program_id(0) == 0)
        def _():
            ang_r = lax.broadcasted_iota(jnp.int32, (tm, 1), 0).astype(F32) * inv_ref[...]
            rc_ref[...] = jnp.cos(ang_r)
            rs_ref[...] = jnp.sin(ang_r)

        ang_0 = (t_in_seq * tm).astype(F32) * inv_ref[...]
        c0, s0 = jnp.cos(ang_0), jnp.sin(ang_0)
        cos = rc_ref[...] * c0 - rs_ref[...] * s0
        sin = (rs_ref[...] * c0 + rc_ref[...] * s0) * sgn_ref[...]

    subs = list(range(0, tm, sub_rows))
    xbs = [(_rms(x_ref[r0:r0 + sub_rows, :]) * g_ref[...]).astype(BF16) for r0 in subs]
    conv_col = 0
    for kind, w_ref, o_ref in zip(kinds, w_refs, o_refs):
        width = w_ref.shape[1]
        ys = [jnp.dot(xb, w_ref[...], preferred_element_type=F32) for xb in xbs]
        for r0, y in zip(subs, ys):
            rs = slice(r0, r0 + sub_rows)
            if kind.startswith("rope"):
                scale = RET_DK ** -0.5 if kind == "rope_q" else 1.0
                for h in range(width // RET_DK):
                    sl = slice(h * RET_DK, (h + 1) * RET_DK)
                    yh = y[:, sl]
                    rot = yh * cos[rs] + pltpu.roll(yh, RET_DK // 2, 1) * sin[rs]
                    o_ref[rs, sl] = (rot * scale).astype(o_ref.dtype)
            elif kind.startswith("conv"):
                for lt in range(width // LANES):
                    ct = conv_col // LANES + lt
                    ls = slice(conv_col + lt * LANES, conv_col + (lt + 1) * LANES)
                    xe_ref[ct, HALO + r0:HALO + r0 + sub_rows, :] = y[:, lt * LANES:(lt + 1) * LANES]
                    acc = None
                    for j in range(CONV_WIDTH):
                        off = HALO - (CONV_WIDTH - 1) + j + r0
                        term = cw_ref[j:j + 1, ls] * xe_ref[ct, off:off + sub_rows, :]
                        acc = term if acc is None else acc + term
                    if kind == "conv_bias":
                        acc = acc + cb_ref[:, ls]
                    o_ref[rs, lt * LANES:(lt + 1) * LANES] = _silu(acc).astype(o_ref.dtype)
            else:
                o_ref[rs, :] = y.astype(o_ref.dtype)
        if kind.startswith("conv"):
            for lt in range(width // LANES):
                ct = conv_col // LANES + lt
                xe_ref[ct, 0:HALO, :] = xe_ref[ct, tm:tm + HALO, :]
            conv_col += width


def _norm_proj(x2d, gain, weights, kinds, seqlen, conv_w=None, conv_b=None, rope=None, out_dtypes=None, tm=512,
               sub_rows=256):
    t, d = x2d.shape
    out_dtypes = out_dtypes or (F32,) * len(weights)
    in_specs = [pl.BlockSpec((tm, d), lambda i: (i, 0)), _full((1, d))] + [_resident(w.shape) for w in weights]
    args = [x2d, gain, *weights]
    scratch = []
    if conv_w is not None:
        in_specs.append(_full(conv_w.shape))
        args.append(conv_w)
        if conv_b is not None:
            in_specs.append(_full(conv_b.shape))
            args.append(conv_b)
        scratch.append(pltpu.VMEM((conv_w.shape[1] // LANES, tm + HALO, LANES), F32))
    if rope is not None:
        in_specs += [_full(r.shape) for r in rope]
        args += list(rope)
        scratch += [pltpu.VMEM((tm, RET_DK), F32), pltpu.VMEM((tm, RET_DK), F32)]
    return pl.pallas_call(
        functools.partial(_norm_proj_body, kinds=tuple(kinds), tiles_per_seq=seqlen // tm, sub_rows=sub_rows),
        grid=(t // tm,),
        in_specs=in_specs,
        out_specs=[pl.BlockSpec((tm, w.shape[1]), lambda i: (i, 0)) for w in weights],
        out_shape=[jax.ShapeDtypeStruct((t, w.shape[1]), dt) for w, dt in zip(weights, out_dtypes)],
        scratch_shapes=scratch,
        compiler_params=_params(1),
        name="norm_proj",
    )(*args)


def _out_mlp_body(x_ref, *refs, n_mix, ff_chunk, final):
    mix_refs, wo_refs = refs[:n_mix], refs[n_mix:2 * n_mix]
    rest = refs[2 * n_mix:]
    if final:
        g_ref, wu_ref, wd_ref, gf_ref, o_ref = rest
    else:
        g_ref, wu_ref, wd_ref, o_ref = rest
    d_ff = wu_ref.shape[1]
    x1 = x_ref[...]
    for m_ref, w_ref in zip(mix_refs, wo_refs):
        x1 = x1 + jnp.dot(m_ref[...].astype(BF16), w_ref[...], preferred_element_type=F32)
    xb = (_rms(x1) * g_ref[...]).astype(BF16)
    mlp = None
    for c in range(d_ff // ff_chunk):
        a = jnp.dot(xb, wu_ref[:, c * ff_chunk:(c + 1) * ff_chunk], preferred_element_type=F32)
        a = jnp.maximum(a, 0.0)
        t = jnp.dot((a * a).astype(BF16), wd_ref[c * ff_chunk:(c + 1) * ff_chunk, :],
                    preferred_element_type=F32)
        mlp = t if mlp is None else mlp + t
    acc = x1 + mlp
    if final:
        acc = _rms(acc) * gf_ref[...]
    o_ref[...] = acc


def _out_mlp(x2d, mixes, w_outs, gain, w_up, w_down, final_gain=None, tm=512, ff_chunk=1024):
    t, d = x2d.shape
    n_mix = len(mixes)
    final = final_gain is not None
    row = lambda w: pl.BlockSpec((tm, w), lambda i: (i, 0))
    in_specs = [row(d)] + [row(m.shape[1]) for m in mixes] + [_resident(w.shape) for w in w_outs]
    in_specs += [_full((1, d)), _resident(w_up.shape), _resident(w_down.shape)]
    args = [x2d, *mixes, *w_outs, gain, w_up, w_down]
    if final:
        in_specs.append(_full((1, d)))
        args.append(final_gain)
    return pl.pallas_call(
        functools.partial(_out_mlp_body, n_mix=n_mix, ff_chunk=ff_chunk, final=final),
        grid=(t // tm,),
        in_specs=in_specs,
        out_specs=row(d),
        out_shape=jax.ShapeDtypeStruct((t, d), F32),
        compiler_params=_params(1),
        name="out_mlp",
    )(*args)


def _retention_body(q_ref, k_ref, v_ref, g_ref, o_ref, s_ref, dm_ref, *, chunk):
    c = chunk
    i = pl.program_id(1)
    gammas = [math.log(1.0 - 2.0 ** (-5.0 - h)) for h in range(RET_HEADS)]

    @pl.when(i == 0)
    def _():
        s_ref[...] = jnp.zeros_like(s_ref)
        rowi = lax.broadcasted_iota(jnp.int32, (c, c), 0)
        coli = lax.broadcasted_iota(jnp.int32, (c, c), 1)
        diff = (rowi - coli).astype(F32)
        for h in range(RET_HEADS):
            dm_ref[h] = jnp.where(rowi >= coli, jnp.exp(gammas[h] * diff), 0.0)

    idx = lax.broadcasted_iota(jnp.int32, (c, 1), 0).astype(F32)
    for h in range(RET_HEADS):
        sl = slice(h * RET_DK, (h + 1) * RET_DK)
        lg = gammas[h]
        qh = q_ref[:, sl]
        kh = k_ref[:, sl]
        vh = v_ref[:, sl]
        scores = _mm_nt(qh, kh) * dm_ref[h]
        state = s_ref[h]
        y = _mm(scores, vh) + _mm(qh * jnp.exp(lg * (idx + 1.0)), state)
        kw = kh * jnp.exp(lg * (c - 1.0 - idx))
        s_ref[h] = state * math.exp(lg * c) + _mm(kw.T, vh)
        gt = g_ref[:, sl]
        o_ref[:, sl] = (_silu(gt) * _rms(y)).astype(o_ref.dtype)


def _retention(q, k, v, gate, bsz, seqlen, chunk=256):
    nt = seqlen // chunk
    w = RET_HEADS * RET_DK
    row = pl.BlockSpec((chunk, w), lambda b, i: (b * nt + i, 0))
    return pl.pallas_call(
        functools.partial(_retention_body, chunk=chunk),
        grid=(bsz, nt),
        in_specs=[row, row, row, row],
        out_specs=row,
        out_shape=jax.ShapeDtypeStruct((bsz * seqlen, w), BF16),
        scratch_shapes=[pltpu.VMEM((RET_HEADS, RET_DK, RET_DK), F32),
                        pltpu.VMEM((RET_HEADS, chunk, chunk), F32)],
        compiler_params=_params(2),
        name="retention",
    )(q, k, v, gate)


def _ssd_body(z_ref, xs_ref, bc_ref, dt_ref, dtb_ref, al_ref, de_ref, nw_ref, e8_ref, o_ref, s_ref, *, chunk):
    c = chunk
    i = pl.program_id(1)
    inner = SSD_HEADS * SSD_P
    gw = inner // SSD_GROUPS
    hpg = SSD_HEADS // SSD_GROUPS

    @pl.when(i == 0)
    def _():
        s_ref[...] = jnp.zeros_like(s_ref)

    xs = xs_ref[...]
    bm = bc_ref[:, :SSD_GROUPS * SSD_N]
    cm = bc_ref[:, SSD_GROUPS * SSD_N:]

    dtv = _softplus(dt_ref[...] + dtb_ref[...])
    la = dtv * (-jnp.exp(al_ref[...]))
    rowi = lax.broadcasted_iota(jnp.int32, (c, c), 0)
    coli = lax.broadcasted_iota(jnp.int32, (c, c), 1)
    causal = rowi >= coli
    cum = _rows01(causal, la)
    cum_t = cum.T
    e8 = e8_ref[...]
    dt_e = _cols01(dtv, e8)
    cum_e = _cols01(cum, e8)
    last_e = cum_e[c - 1:c, :]
    xd = xs * dt_e
    xdw = xd * jnp.exp(last_e - cum_e)
    lane = lax.broadcasted_iota(jnp.int32, (1, gw), 1)

    ys = []
    for g in range(SSD_GROUPS):
        bg = bm[:, g * SSD_N:(g + 1) * SSD_N]
        cg = cm[:, g * SSD_N:(g + 1) * SSD_N]
        gsl = slice(g * gw, (g + 1) * gw)
        cb = _mm_nt(cg, bg)
        xd_g = xd[:, gsl]
        yd = None
        for j in range(hpg):
            h = g * hpg + j
            seg = cum[:, h:h + 1] - cum_t[h:h + 1, :]
            lmat = jnp.exp(jnp.where(causal, seg, -jnp.inf))
            xm = jnp.where((lane >= SSD_P * j) & (lane < SSD_P * (j + 1)), xd_g, 0.0)
            t = _mm(cb * lmat, xm)
            yd = t if yd is None else yd + t
        state = s_ref[g]
        y_off = _mm(cg, state) * jnp.exp(cum_e[:, gsl])
        s_ref[g] = state * jnp.exp(last_e[:, gsl]) + _mm(bg.astype(F32).T, xdw[:, gsl])
        ys.append(yd + y_off)

    zz = z_ref[...]
    for g in range(SSD_GROUPS):
        gsl = slice(g * gw, (g + 1) * gw)
        yg = (ys[g] + de_ref[:, gsl] * xs[:, gsl]) * _silu(zz[:, gsl])
        o_ref[:, gsl] = (_rms(yg) * nw_ref[:, gsl]).astype(o_ref.dtype)


def _ssd(z, xs, bc, dt, dt_bias, a_log, d_e, norm_w, e8, bsz, seqlen, chunk=256):
    nt = seqlen // chunk
    inner = SSD_HEADS * SSD_P
    row = lambda w: pl.BlockSpec((chunk, w), lambda b, i: (b * nt + i, 0))
    return pl.pallas_call(
        functools.partial(_ssd_body, chunk=chunk),
        grid=(bsz, nt),
        in_specs=[row(inner), row(inner), row(bc.shape[1]), row(LANES), _full((1, LANES)), _full((1, LANES)),
                  _full((1, inner)), _full((1, inner)), _full(e8.shape)],
        out_specs=row(inner),
        out_shape=jax.ShapeDtypeStruct((bsz * seqlen, inner), BF16),
        scratch_shapes=[pltpu.VMEM((SSD_GROUPS, SSD_N, inner // SSD_GROUPS), F32)],
        compiler_params=_params(2),
        name="ssd",
    )(z, xs, bc, dt, dt_bias, a_log, d_e, norm_w, e8)


def _gdn_body(q_ref, k_ref, v_ref, z_ref, b_ref, a_ref, al_ref, dtb_ref, nw_ref, e6_ref, o_ref, s_ref, *, tile):
    bsz = q_ref.shape[0]
    ck = GDN_CHUNK
    nck = tile // ck
    nall = bsz * nck
    i = pl.program_id(0)
    qk_w = GDN_HEADS * GDN_DK
    rows_of = lambda ref: jnp.concatenate([ref[b] for b in range(bsz)], axis=0)

    @pl.when(i == 0)
    def _():
        s_ref[...] = jnp.zeros_like(s_ref)

    beta = _sigmoid(rows_of(b_ref))
    gl = -jnp.exp(al_ref[...]) * _softplus(rows_of(a_ref) + dtb_ref[...])

    rowt = lax.broadcasted_iota(jnp.int32, (tile, tile), 0)
    colt = lax.broadcasted_iota(jnp.int32, (tile, tile), 1)
    chunk_causal = ((rowt >> 6) == (colt >> 6)) & (rowt >= colt)
    gc = jnp.concatenate([_rows01(chunk_causal, gl[b * tile:(b + 1) * tile]) for b in range(bsz)], axis=0)
    gc_t = gc.T
    e6 = e6_ref[...]
    gc_e = _cols01(gc, e6)
    beta_e = _cols01(beta, e6)
    eg_e = jnp.exp(gc_e)
    glast_rows = [jnp.broadcast_to(gc_e[(cix + 1) * ck - 1:(cix + 1) * ck, :], (ck, qk_w)) for cix in range(nall)]
    glast_e = jnp.concatenate(glast_rows, axis=0)
    tail_e = jnp.exp(glast_e - gc_e)
    dlast_e = jnp.exp(glast_e)

    rowi = lax.broadcasted_iota(jnp.int32, (ck, ck), 0)
    coli = lax.broadcasted_iota(jnp.int32, (ck, ck), 1)
    causal = rowi >= coli
    strict = rowi > coli

    def same_block(bits):
        return (rowi >> bits) == (coli >> bits)

    eye = (rowi == coli).astype(F32)
    base_mask = same_block(3) & strict
    level_masks = [same_block(b + 1) & (~same_block(b)) & strict for b in (3, 4, 5)]

    q_all, k_all, v_all = rows_of(q_ref), rows_of(k_ref), rows_of(v_ref)
    q_l, k_l, kb_l, rhs_l, kt_l, qg_l, decay_l = [], [], [], [], [], [], []
    for h in range(GDN_HEADS):
        sl = slice(h * GDN_DK, (h + 1) * GDN_DK)
        qh = q_all[:, sl]
        kh = k_all[:, sl]
        vh = v_all[:, sl]
        qh = qh * lax.rsqrt(jnp.sum(qh * qh, axis=-1, keepdims=True) + EPS) * (GDN_DK ** -0.5)
        kh = kh * lax.rsqrt(jnp.sum(kh * kh, axis=-1, keepdims=True) + EPS)
        bh = beta_e[:, sl]
        kb = kh * bh
        rhs = jnp.concatenate([vh * bh, kb * eg_e[:, sl]], axis=1)
        kt = kh * tail_e[:, sl]
        qg = qh * eg_e[:, sl]
        for cix in range(nall):
            rs = slice(cix * ck, (cix + 1) * ck)
            q_l.append(qh[rs])
            k_l.append(kh[rs])
            kb_l.append(kb[rs])
            rhs_l.append(rhs[rs])
            kt_l.append(kt[rs])
            qg_l.append(qg[rs])
            seg = gc[rs, h:h + 1] - gc_t[h:h + 1, rs]
            decay_l.append(jnp.exp(jnp.where(causal, seg, -jnp.inf)))
    q3, k3, kb3 = jnp.stack(q_l), jnp.stack(k_l), jnp.stack(kb_l)
    decay = jnp.stack(decay_l)
    lower = jnp.where(strict, _bmm_nt(kb3, k3) * decay, 0.0)
    attn = _bmm_nt(q3, k3) * decay

    d8 = jnp.where(base_mask, lower, 0.0)
    x2 = _bmm(d8, d8)
    x4 = _bmm(x2, x2)
    tinv = eye - d8
    tinv = tinv + _bmm(tinv, x2)
    tinv = tinv + _bmm(tinv, x4)
    for lm in level_masks:
        e = jnp.where(lm, lower, 0.0)
        tinv = tinv - _bmm(tinv, _bmm(e, tinv))

    uw = _bmm(tinv, jnp.stack(rhs_l))

    def of_chunk(xs, cix):
        return jnp.stack([xs[h * nall + b * nck + cix] for b in range(bsz) for h in range(GDN_HEADS)])

    state = s_ref[...]
    outs = []
    for cix in range(nck):
        uw_c = of_chunk(uw, cix)
        r = _bmm(jnp.concatenate([uw_c[:, :, GDN_DK:], of_chunk(qg_l, cix)], axis=1), state)
        v_new = uw_c[:, :, :GDN_DK] - r[:, :ck]
        outs.append(r[:, ck:] + _bmm(of_chunk(attn, cix), v_new))
        kt_t = jnp.stack([kt_l[h * nall + b * nck + cix].T for b in range(bsz) for h in range(GDN_HEADS)])
        dl = jnp.stack([dlast_e[(b * nck + cix) * ck:(b * nck + cix) * ck + 1, h * GDN_DK:(h + 1) * GDN_DK]
                        for b in range(bsz) for h in range(GDN_HEADS)])
        state = state * dl + _bmm(kt_t, v_new)
    s_ref[...] = state
    for b in range(bsz):
        for h in range(GDN_HEADS):
            sl = slice(h * GDN_DK, (h + 1) * GDN_DK)
            o = jnp.concatenate([outs[cix][b * GDN_HEADS + h] for cix in range(nck)], axis=0)
            o_ref[b, :, sl] = (_rms(o) * nw_ref[...] * _silu(z_ref[b, :, sl])).astype(o_ref.dtype)


def _gdn(q, k, v, z, b_raw, a_raw, a_log, dt_bias, norm_w, e6, bsz, seqlen, tile=256):
    nt = seqlen // tile
    wv = GDN_HEADS * GDN_DK
    row = lambda w: pl.BlockSpec((bsz, tile, w), lambda i: (0, i, 0))
    seq = lambda a: a.reshape(bsz, seqlen, a.shape[-1])
    return pl.pallas_call(
        functools.partial(_gdn_body, tile=tile),
        grid=(nt,),
        in_specs=[row(wv), row(wv), row(wv), row(wv), row(LANES), row(LANES), _full((1, LANES)), _full((1, LANES)),
                  _full((1, GDN_DK)), _full(e6.shape)],
        out_specs=row(wv),
        out_shape=jax.ShapeDtypeStruct((bsz, seqlen, wv), BF16),
        scratch_shapes=[pltpu.VMEM((bsz * GDN_HEADS, GDN_DK, GDN_DK), F32)],
        compiler_params=_params(1),
        name="gdn",
    )(seq(q), seq(k), seq(v), seq(z), seq(b_raw), seq(a_raw), a_log, dt_bias, norm_w, e6).reshape(bsz * seqlen, wv)


def _cmul_add(ar, ai, br, bi, cr, ci):
    return ar * br - ai * bi + cr, ar * bi + ai * br + ci


def _s5_body(u_ref, are_ref, aim_ref, ls_ref, bre_ref, bim_ref, cre_ref, cim_ref, d_ref, wg_ref, bg_ref,
             o_ref, pre_ref, pim_ref, qre_ref, qim_ref, bsr_ref, bsi_ref, h_ref, hre_ref, him_ref, *, tile):
    bsz = u_ref.shape[0]
    tt = tile
    sub = 8
    seg = tt // sub
    seg_bits = seg.bit_length() - 1
    ns = S5_GROUPS * S5_STATE
    i = pl.program_id(0)

    @pl.when(i == 0)
    def _():
        step = jnp.exp(ls_ref[...])

        def powers(n, scale):
            tpos = (lax.broadcasted_iota(jnp.int32, (n, 1), 0).astype(F32) + 1.0) * scale
            mag = jnp.exp(tpos * (are_ref[...] * step))
            ang = tpos * (aim_ref[...] * step)
            return mag * jnp.cos(ang), mag * jnp.sin(ang)

        pre_ref[...], pim_ref[...] = powers(seg, 1.0)
        qre_ref[...], qim_ref[...] = powers(sub, float(seg))
        h_ref[...] = jnp.zeros_like(h_ref)
        a_re, a_im = are_ref[...], aim_ref[...]
        n_re, n_im = pre_ref[0:1, :] - 1.0, pim_ref[0:1, :]
        den = a_re * a_re + a_im * a_im
        c_re = (n_re * a_re + n_im * a_im) / den
        c_im = (n_im * a_re - n_re * a_im) / den
        b_re, b_im = bre_ref[...], bim_ref[...]
        bsr_ref[...] = (c_re * b_re - c_im * b_im).astype(BF16)
        bsi_ref[...] = (c_re * b_im + c_im * b_re).astype(BF16)

    row_i = lax.broadcasted_iota(jnp.int32, (tt, tt), 0)
    col_i = lax.broadcasted_iota(jnp.int32, (tt, tt), 1)
    perm = col_i == (row_i & (sub - 1)) * seg + (row_i >> 3)
    unperm = col_i == (row_i & (seg - 1)) * sub + (row_i >> seg_bits)
    u = jnp.concatenate([_rows01(perm, u_ref[b]) for b in range(bsz)], axis=0)

    hre_ref[...] = _mm(u, bsr_ref[...])
    him_ref[...] = _mm(u, bsi_ref[...])

    sl_i = lax.broadcasted_iota(jnp.int32, (sub, LANES), 0)

    def down(x, d, fill):
        return jnp.where(sl_i >= d, pltpu.roll(x, d, 0), fill)

    for b in range(bsz):
        r0 = b * tt
        for j in range(ns // LANES):
            ls = slice(j * LANES, (j + 1) * LANES)
            ar, ai = pre_ref[0:1, ls], pim_ref[0:1, ls]
            gr, gi = hre_ref[r0:r0 + sub, ls], him_ref[r0:r0 + sub, ls]
            for t in range(1, seg):
                rs = slice(r0 + t * sub, r0 + (t + 1) * sub)
                gr, gi = _cmul_add(ar, ai, gr, gi, hre_ref[rs, ls], him_ref[rs, ls])
                hre_ref[rs, ls] = gr
                him_ref[rs, ls] = gi
            fr, fi = gr, gi
            for k in range(3):
                d = 1 << k
                mr, mi = qre_ref[d - 1:d, ls], qim_ref[d - 1:d, ls]
                fr, fi = _cmul_add(mr, mi, down(fr, d, 0.0), down(fi, d, 0.0), fr, fi)
            cr, ci = h_ref[2 * b:2 * b + 1, ls], h_ref[2 * b + 1:2 * b + 2, ls]
            zr, zi = _cmul_add(qre_ref[:, ls], qim_ref[:, ls], cr, ci, fr, fi)
            h_ref[2 * b:2 * b + 1, ls] = zr[sub - 1:sub, :]
            h_ref[2 * b + 1:2 * b + 2, ls] = zi[sub - 1:sub, :]
            er, ei = down(zr, 1, cr), down(zi, 1, ci)
            for t in range(seg):
                rs = slice(r0 + t * sub, r0 + (t + 1) * sub)
                hr, hi = _cmul_add(pre_ref[t:t + 1, ls], pim_ref[t:t + 1, ls], er, ei,
                                   hre_ref[rs, ls], him_ref[rs, ls])
                hre_ref[rs, ls] = hr
                him_ref[rs, ls] = hi

    y = _mm(hre_ref[...], cre_ref[...]) - _mm(him_ref[...], cim_ref[...]) + d_ref[...] * u
    y = jax.nn.gelu(y)
    out = y * _sigmoid(_mm(y, wg_ref[...]) + bg_ref[...])
    for b in range(bsz):
        o_ref[b] = _mm(unperm, out[b * tt:(b + 1) * tt]).astype(o_ref.dtype)


def _s5(u, a_re, a_im, log_step, b_re, b_im, c_re, c_im, d_skip, w_glu, b_glu, bsz, seqlen, tile=256):
    nt = seqlen // tile
    ns = S5_GROUPS * S5_STATE
    row = pl.BlockSpec((bsz, tile, S5_CH), lambda i: (0, i, 0))
    vec = _full((1, ns))
    carry_rows = HALO * pl.cdiv(2 * bsz, HALO)
    return pl.pallas_call(
        functools.partial(_s5_body, tile=tile),
        grid=(nt,),
        in_specs=[row, vec, vec, vec, _full((S5_CH, ns)), _full((S5_CH, ns)), _full((ns, S5_CH)),
                  _full((ns, S5_CH)), _full((1, S5_CH)), _full((S5_CH, S5_CH)), _full((1, S5_CH))],
        out_specs=row,
        out_shape=jax.ShapeDtypeStruct((bsz, seqlen, S5_CH), BF16),
        scratch_shapes=[pltpu.VMEM((tile // 8, ns), F32), pltpu.VMEM((tile // 8, ns), F32),
                        pltpu.VMEM((8, ns), F32), pltpu.VMEM((8, ns), F32),
                        pltpu.VMEM((S5_CH, ns), BF16), pltpu.VMEM((S5_CH, ns), BF16),
                        pltpu.VMEM((carry_rows, ns), F32),
                        pltpu.VMEM((bsz * tile, ns), F32), pltpu.VMEM((bsz * tile, ns), F32)],
        compiler_params=_params(1),
        name="s5",
    )(u.reshape(bsz, seqlen, S5_CH), a_re, a_im, log_step, b_re, b_im, c_re, c_im, d_skip, w_glu, b_glu
      ).reshape(bsz * seqlen, S5_CH)


def _pad_lanes(v, width=LANES):
    v = v.reshape(1, -1).astype(F32)
    return jnp.pad(v, ((0, 0), (0, width - v.shape[1])))


def _pad_cols(w, width=LANES):
    return jnp.pad(w, ((0, 0), (0, width - w.shape[1])))


def _head_expander(n_heads, head_width, rows=LANES):
    r = jnp.arange(rows)[:, None]
    col_head = jnp.arange(n_heads * head_width)[None, :] // head_width
    return (r == col_head).astype(BF16)


def _block_diag(blocks):
    g, r, c = blocks.shape
    eye = jnp.eye(g, dtype=blocks.dtype)
    return (eye[:, None, :, None] * blocks[:, :, None, :]).reshape(g * r, g * c)


def kernel(x, l0_norm_mix, l0_w_in, ssd_conv_w, ssd_conv_b, ssd_dt_bias, ssd_A_log, ssd_D, ssd_norm_w, l0_w_out, l0_norm_mlp, l0_w_up, l0_w_down, l1_norm_mix, l1_w_in, gdn_conv_w, gdn_A_log, gdn_dt_bias, gdn_norm_w, s5_A_re, s5_A_im, s5_log_step, s5_B_re, s5_B_im, s5_C_re, s5_C_im, s5_D, s5_w_glu, s5_b_glu, l1_w_out, l1_norm_mlp, l1_w_up, l1_w_down, final_norm):
    bsz, seqlen, d = x.shape
    x2d = x.reshape(bsz * seqlen, d)
    row = lambda v: v.reshape(1, -1).astype(F32)
    bf = lambda w: w.astype(BF16)

    ret_w = RET_HEADS * RET_DK
    ssd_inner = SSD_HEADS * SSD_P
    ssd_xbc = ssd_inner + 2 * SSD_GROUPS * SSD_N
    offs, acc = [], 0
    for wdt in (ret_w, ret_w, ret_w, ret_w, ssd_inner, ssd_inner, ssd_xbc - ssd_inner, SSD_HEADS):
        offs.append((acc, acc + wdt))
        acc += wdt
    wq, wk, wv, wgate, wz, wxs, wbc, wdt_ = [l0_w_in[:, a:b] for a, b in offs]
    half = RET_DK // 2
    inv = ROPE_THETA ** (-jnp.arange(half, dtype=F32) / half)
    inv2 = jnp.concatenate([inv, inv]).reshape(1, RET_DK)
    sgn = jnp.concatenate([-jnp.ones((half,), F32), jnp.ones((half,), F32)]).reshape(1, RET_DK)
    xs, bc, q, k, v, gate, z, dt = _norm_proj(
        x2d, row(l0_norm_mix), [bf(w) for w in (wxs, wbc, wq, wk, wv, wgate, wz, _pad_cols(wdt_))],
        ("conv_bias", "conv_bias", "rope_q", "rope_k", "plain", "plain", "plain", "plain"), seqlen,
        conv_w=ssd_conv_w.astype(F32), conv_b=row(ssd_conv_b), rope=(inv2, sgn),
        out_dtypes=(F32, BF16, BF16, BF16, BF16, F32, F32, F32))

    ret_out = _retention(q, k, v, gate, bsz, seqlen)
    ssd_out = _ssd(z, xs, bc, dt, _pad_lanes(ssd_dt_bias), _pad_lanes(ssd_A_log),
                   row(jnp.repeat(ssd_D.astype(F32), SSD_P)), row(ssd_norm_w),
                   _head_expander(SSD_HEADS, SSD_P), bsz, seqlen)

    x2d = _out_mlp(x2d, [ret_out, ssd_out], [bf(l0_w_out[:ret_w]), bf(l0_w_out[ret_w:])],
                   row(l0_norm_mlp), bf(l0_w_up), bf(l0_w_down))

    gdn_w = GDN_HEADS * GDN_DK
    offs, acc = [], 0
    for wdt in (gdn_w, gdn_w, gdn_w, gdn_w, GDN_HEADS, GDN_HEADS, S5_CH):
        offs.append((acc, acc + wdt))
        acc += wdt
    w1 = [l1_w_in[:, a:b] for a, b in offs]
    w1[4] = _pad_cols(w1[4])
    w1[5] = _pad_cols(w1[5])
    qg, kg, vg, zg, b_raw, a_raw, u = _norm_proj(
        x2d, row(l1_norm_mix), [bf(w) for w in w1],
        ("conv", "conv", "conv", "plain", "plain", "plain", "plain"), seqlen, conv_w=gdn_conv_w.astype(F32))

    gdn_out = _gdn(qg, kg, vg, zg, b_raw, a_raw, _pad_lanes(gdn_A_log), _pad_lanes(gdn_dt_bias), row(gdn_norm_w),
                   _head_expander(GDN_HEADS, GDN_DK), bsz, seqlen)

    flat = lambda p: p.reshape(1, S5_GROUPS * S5_STATE).astype(F32)
    b_re = _block_diag(jnp.swapaxes(s5_B_re.astype(F32), 1, 2))
    b_im = _block_diag(jnp.swapaxes(s5_B_im.astype(F32), 1, 2))
    c_re = _block_diag(jnp.swapaxes(s5_C_re.astype(F32), 1, 2))
    c_im = _block_diag(jnp.swapaxes(s5_C_im.astype(F32), 1, 2))
    s5_out = _s5(u, flat(s5_A_re), flat(s5_A_im), row(jnp.repeat(s5_log_step.astype(F32), S5_STATE)),
                 b_re, b_im, bf(c_re), bf(c_im), row(s5_D), bf(s5_w_glu), row(s5_b_glu), bsz, seqlen)

    out = _out_mlp(x2d, [gdn_out, s5_out], [bf(l1_w_out[:gdn_w]), bf(l1_w_out[gdn_w:])],
                   row(l1_norm_mlp), bf(l1_w_up), bf(l1_w_down), final_gain=row(final_norm))
    return out.reshape(bsz, seqlen, d)
```

```python
import functools
import math

import jax
import jax.numpy as jnp
from jax import lax
from jax.experimental import pallas as pl
from jax.experimental.pallas import tpu as pltpu

F32 = jnp.float32
BF16 = jnp.bfloat16
EPS = 1e-6
CONV_WIDTH = 4
HALO = 8

RET_HEADS, RET_DK = 4, 128
SSD_HEADS, SSD_P, SSD_GROUPS, SSD_N = 8, 64, 2, 128
GDN_HEADS, GDN_DK = 6, 128
GDN_CHUNK = 64
S5_CH, S5_GROUPS, S5_GROUP, S5_STATE = 256, 16, 16, 64
ROPE_THETA = 10000.0
LANES = 128

VMEM_LIMIT = 56 * 1024 * 1024


def _full(shape):
    nd = len(shape)
    return pl.BlockSpec(shape, lambda *_: (0,) * nd)


def _resident(shape):
    nd = len(shape)
    return pl.BlockSpec(shape, lambda *_: (0,) * nd, pipeline_mode=pl.Buffered(1))


def _params(n_axes):
    return pltpu.CompilerParams(dimension_semantics=("arbitrary",) * n_axes, vmem_limit_bytes=VMEM_LIMIT)


def _mm(a, b):
    return jnp.dot(a.astype(BF16), b.astype(BF16), preferred_element_type=F32)


def _mm_nt(a, b):
    return lax.dot_general(a.astype(BF16), b.astype(BF16), (((1,), (1,)), ((), ())), preferred_element_type=F32)


def _bmm(a, b):
    return lax.dot_general(a.astype(BF16), b.astype(BF16), (((2,), (1,)), ((0,), (0,))), preferred_element_type=F32)


def _bmm_nt(a, b):
    return lax.dot_general(a.astype(BF16), b.astype(BF16), (((2,), (2,)), ((0,), (0,))), preferred_element_type=F32)


def _split3(a):
    hi = a.astype(BF16)
    r1 = a - hi.astype(F32)
    mid = r1.astype(BF16)
    lo = (r1 - mid.astype(F32)).astype(BF16)
    return hi, mid, lo


def _rows01(m01, a):
    n = a.shape[1]
    p = jnp.dot(m01.astype(BF16), jnp.concatenate(_split3(a), axis=1), preferred_element_type=F32)
    return p[:, :n] + p[:, n:2 * n] + p[:, 2 * n:]


def _cols01(a, e01):
    hi, mid, lo = _split3(a)
    dot = lambda x: jnp.dot(x, e01, preferred_element_type=F32)
    return dot(hi) + dot(mid) + dot(lo)


def _sigmoid(x):
    return 1.0 / (1.0 + jnp.exp(-x))


def _silu(x):
    return x * _sigmoid(x)


def _softplus(x):
    return jnp.maximum(x, 0.0) + jnp.log(1.0 + jnp.exp(-jnp.abs(x)))


def _rms(x):
    return x * lax.rsqrt(jnp.mean(x * x, axis=-1, keepdims=True) + EPS)


def _norm_proj_body(*refs, kinds, tiles_per_seq, sub_rows):
    it = iter(refs)
    x_ref, g_ref = next(it), next(it)
    w_refs = [next(it) for _ in kinds]
    has_conv = any(k.startswith("conv") for k in kinds)
    has_rope = any(k.startswith("rope") for k in kinds)
    cw_ref = next(it) if has_conv else None
    cb_ref = next(it) if "conv_bias" in kinds else None
    inv_ref, sgn_ref = (next(it), next(it)) if has_rope else (None, None)
    o_refs = [next(it) for _ in kinds]
    xe_ref = next(it) if has_conv else None
    rc_ref, rs_ref = (next(it), next(it)) if has_rope else (None, None)

    tm = x_ref.shape[0]
    t_in_seq = pl.program_id(0) % tiles_per_seq

    if has_conv:
        @pl.when(t_in_seq == 0)
        def _():
            xe_ref[:, 0:HALO, :] = jnp.zeros((xe_ref.shape[0], HALO, LANES), F32)

    if has_rope:
        @pl.when(pl.program_id(0) == 0)
        def _():
            ang_r = lax.broadcasted_iota(jnp.int32, (tm, 1), 0).astype(F32) * inv_ref[...]
            rc_ref[...] = jnp.cos(ang_r)
            rs_ref[...] = jnp.sin(ang_r)

        ang_0 = (t_in_seq * tm).astype(F32) * inv_ref[...]
        c0, s0 = jnp.cos(ang_0), jnp.sin(ang_0)
        cos = rc_ref[...] * c0 - rs_ref[...] * s0
        sin = (rs_ref[...] * c0 + rc_ref[...] * s0) * sgn_ref[...]

    subs = list(range(0, tm, sub_rows))
    xbs = [(_rms(x_ref[r0:r0 + sub_rows, :]) * g_ref[...]).astype(BF16) for r0 in subs]
    conv_col = 0
    for kind, w_ref, o_ref in zip(kinds, w_refs, o_refs):
        width = w_ref.shape[1]
        ys = [jnp.dot(xb, w_ref[...], preferred_element_type=F32) for xb in xbs]
        for r0, y in zip(subs, ys):
            rs = slice(r0, r0 + sub_rows)
            if kind.startswith("rope"):
                scale = RET_DK ** -0.5 if kind == "rope_q" else 1.0
                for h in range(width // RET_DK):
                    sl = slice(h * RET_DK, (h + 1) * RET_DK)
                    yh = y[:, sl]
                    rot = yh * cos[rs] + pltpu.roll(yh, RET_DK // 2, 1) * sin[rs]
                    o_ref[rs, sl] = (rot * scale).astype(o_ref.dtype)
            elif kind.startswith("conv"):
                for lt in range(width // LANES):
                    ct = conv_col // LANES + lt
                    ls = slice(conv_col + lt * LANES, conv_col + (lt + 1) * LANES)
                    xe_ref[ct, HALO + r0:HALO + r0 + sub_rows, :] = y[:, lt * LANES:(lt + 1) * LANES]
                    acc = None
                    for j in range(CONV_WIDTH):
                        off = HALO - (CONV_WIDTH - 1) + j + r0
                        term = cw_ref[j:j + 1, ls] * xe_ref[ct, off:off + sub_rows, :]
                        acc = term if acc is None else acc + term
                    if kind == "conv_bias":
                        acc = acc + cb_ref[:, ls]
                    o_ref[rs, lt * LANES:(lt + 1) * LANES] = _silu(acc).astype(o_ref.dtype)
            else:
                o_ref[rs, :] = y.astype(o_ref.dtype)
        if kind.startswith("conv"):
            for lt in range(width // LANES):
                ct = conv_col // LANES + lt
                xe_ref[ct, 0:HALO, :] = xe_ref[ct, tm:tm + HALO, :]
            conv_col += width


def _norm_proj(x2d, gain, weights, kinds, seqlen, conv_w=None, conv_b=None, rope=None, out_dtypes=None, tm=512,
               sub_rows=256):
    t, d = x2d.shape
    out_dtypes = out_dtypes or (F32,) * len(weights)
    in_specs = [pl.BlockSpec((tm, d), lambda i: (i, 0)), _full((1, d))] + [_resident(w.shape) for w in weights]
    args = [x2d, gain, *weights]
    scratch = []
    if conv_w is not None:
        in_specs.append(_full(conv_w.shape))
        args.append(conv_w)
        if conv_b is not None:
            in_specs.append(_full(conv_b.shape))
            args.append(conv_b)
        scratch.append(pltpu.VMEM((conv_w.shape[1] // LANES, tm + HALO, LANES), F32))
    if rope is not None:
        in_specs += [_full(r.shape) for r in rope]
        args += list(rope)
        scratch += [pltpu.VMEM((tm, RET_DK), F32), pltpu.VMEM((tm, RET_DK), F32)]
    return pl.pallas_call(
        functools.partial(_norm_proj_body, kinds=tuple(kinds), tiles_per_seq=seqlen // tm, sub_rows=sub_rows),
        grid=(t // tm,),
        in_specs=in_specs,
        out_specs=[pl.BlockSpec((tm, w.shape[1]), lambda i: (i, 0)) for w in weights],
        out_shape=[jax.ShapeDtypeStruct((t, w.shape[1]), dt) for w, dt in zip(weights, out_dtypes)],
        scratch_shapes=scratch,
        compiler_params=_params(1),
        name="norm_proj",
    )(*args)


def _out_mlp_body(x_ref, *refs, n_mix, ff_chunk, final):
    mix_refs, wo_refs = refs[:n_mix], refs[n_mix:2 * n_mix]
    rest = refs[2 * n_mix:]
    if final:
        g_ref, wu_ref, wd_ref, gf_ref, o_ref = rest
    else:
        g_ref, wu_ref, wd_ref, o_ref = rest
    d_ff = wu_ref.shape[1]
    x1 = x_ref[...]
    for m_ref, w_ref in zip(mix_refs, wo_refs):
        x1 = x1 + jnp.dot(m_ref[...].astype(BF16), w_ref[...], preferred_element_type=F32)
    xb = (_rms(x1) * g_ref[...]).astype(BF16)
    mlp = None
    for c in range(d_ff // ff_chunk):
        a = jnp.dot(xb, wu_ref[:, c * ff_chunk:(c + 1) * ff_chunk], preferred_element_type=F32)
        a = jnp.maximum(a, 0.0)
        t = jnp.dot((a * a).astype(BF16), wd_ref[c * ff_chunk:(c + 1) * ff_chunk, :],
                    preferred_element_type=F32)
        mlp = t if mlp is None else mlp + t
    acc = x1 + mlp
    if final:
        acc = _rms(acc) * gf_ref[...]
    o_ref[...] = acc


def _out_mlp(x2d, mixes, w_outs, gain, w_up, w_down, final_gain=None, tm=512, ff_chunk=1024):
    t, d = x2d.shape
    n_mix = len(mixes)
    final = final_gain is not None
    row = lambda w: pl.BlockSpec((tm, w), lambda i: (i, 0))
    in_specs = [row(d)] + [row(m.shape[1]) for m in mixes] + [_resident(w.shape) for w in w_outs]
    in_specs += [_full((1, d)), _resident(w_up.shape), _resident(w_down.shape)]
    args = [x2d, *mixes, *w_outs, gain, w_up, w_down]
    if final:
        in_specs.append(_full((1, d)))
        args.append(final_gain)
    return pl.pallas_call(
        functools.partial(_out_mlp_body, n_mix=n_mix, ff_chunk=ff_chunk, final=final),
        grid=(t // tm,),
        in_specs=in_specs,
        out_specs=row(d),
        out_shape=jax.ShapeDtypeStruct((t, d), F32),
        compiler_params=_params(1),
        name="out_mlp",
    )(*args)


def _retention_body(q_ref, k_ref, v_ref, g_ref, o_ref, s_ref, dm_ref, *, chunk):
    bsz = q_ref.shape[0]
    c = chunk
    i = pl.program_id(0)
    gammas = [math.log(1.0 - 2.0 ** (-5.0 - h)) for h in range(RET_HEADS)]

    @pl.when(i == 0)
    def _():
        s_ref[...] = jnp.zeros_like(s_ref)
        rowi = lax.broadcasted_iota(jnp.int32, (c, c), 0)
        coli = lax.broadcasted_iota(jnp.int32, (c, c), 1)
        diff = (rowi - coli).astype(F32)
        for h in range(RET_HEADS):
            dm_ref[h] = jnp.where(rowi >= coli, jnp.exp(gammas[h] * diff), 0.0)

    pbh = [(b, h) for b in range(bsz) for h in range(RET_HEADS)]
    head = lambda ref, b, h: ref[b, :, h * RET_DK:(h + 1) * RET_DK]
    idx = lax.broadcasted_iota(jnp.int32, (c, 1), 0).astype(F32)
    q8 = jnp.stack([head(q_ref, b, h) for b, h in pbh])
    k8 = jnp.stack([head(k_ref, b, h) for b, h in pbh])
    v8 = jnp.stack([head(v_ref, b, h) for b, h in pbh])
    scores = _bmm_nt(q8, k8) * jnp.stack([dm_ref[h] for _, h in pbh])
    q_dec = jnp.stack([jnp.exp(gammas[h] * (idx + 1.0)) for _, h in pbh])
    state = s_ref[...]
    y = _bmm(scores, v8) + _bmm(q8 * q_dec, state)
    kw_t = jnp.stack([(head(k_ref, b, h) * jnp.exp(gammas[h] * (c - 1.0 - idx))).T for b, h in pbh])
    chunk_decay = jnp.stack([jnp.full((1, RET_DK), math.exp(gammas[h] * c), F32) for _, h in pbh])
    s_ref[...] = state * chunk_decay + _bmm(kw_t, v8)
    for p, (b, h) in enumerate(pbh):
        gt = head(g_ref, b, h).astype(F32)
        o_ref[b, :, h * RET_DK:(h + 1) * RET_DK] = (_silu(gt) * _rms(y[p])).astype(o_ref.dtype)


def _retention(q, k, v, gate, bsz, seqlen, chunk=256):
    nt = seqlen // chunk
    w = RET_HEADS * RET_DK
    row = pl.BlockSpec((bsz, chunk, w), lambda i: (0, i, 0))
    seq = lambda a: a.reshape(bsz, seqlen, w)
    return pl.pallas_call(
        functools.partial(_retention_body, chunk=chunk),
        grid=(nt,),
        in_specs=[row, row, row, row],
        out_specs=row,
        out_shape=jax.ShapeDtypeStruct((bsz, seqlen, w), BF16),
        scratch_shapes=[pltpu.VMEM((bsz * RET_HEADS, RET_DK, RET_DK), F32),
                        pltpu.VMEM((RET_HEADS, chunk, chunk), F32)],
        compiler_params=_params(1),
        name="retention",
    )(seq(q), seq(k), seq(v), seq(gate)).reshape(bsz * seqlen, w)


def _ssd_body(z_ref, xs_ref, bc_ref, dt_ref, dtb_ref, al_ref, de_ref, nw_ref, e8_ref, o_ref, s_ref, *, chunk):
    bsz = xs_ref.shape[0]
    c = chunk
    i = pl.program_id(0)
    inner = SSD_HEADS * SSD_P
    gw = inner // SSD_GROUPS
    hpg = SSD_HEADS // SSD_GROUPS
    rows_of = lambda ref: jnp.concatenate([ref[b] for b in range(bsz)], axis=0)

    @pl.when(i == 0)
    def _():
        s_ref[...] = jnp.zeros_like(s_ref)

    xs = rows_of(xs_ref)
    dtv = _softplus(rows_of(dt_ref) + dtb_ref[...])
    la = dtv * (-jnp.exp(al_ref[...]))
    rowi = lax.broadcasted_iota(jnp.int32, (c, c), 0)
    coli = lax.broadcasted_iota(jnp.int32, (c, c), 1)
    causal = rowi >= coli
    cums = [_rows01(causal, la[b * c:(b + 1) * c]) for b in range(bsz)]
    cum_ts = [cm_.T for cm_ in cums]
    e8 = e8_ref[...]
    dt_e = _cols01(dtv, e8)
    cum_e = _cols01(jnp.concatenate(cums, axis=0), e8)
    last_e = jnp.concatenate([jnp.broadcast_to(cum_e[(b + 1) * c - 1:(b + 1) * c, :], (c, inner)) for b in range(bsz)],
                             axis=0)
    xd = xs * dt_e
    xdw = xd * jnp.exp(last_e - cum_e)
    lane = lax.broadcasted_iota(jnp.int32, (1, gw), 1)

    pb = [(b, g) for b in range(bsz) for g in range(SSD_GROUPS)]
    rs = lambda b: slice(b * c, (b + 1) * c)
    c_off = SSD_GROUPS * SSD_N
    bgs = jnp.stack([bc_ref[b, :, g * SSD_N:(g + 1) * SSD_N] for b, g in pb])
    cgs = jnp.stack([bc_ref[b, :, c_off + g * SSD_N:c_off + (g + 1) * SSD_N] for b, g in pb])
    cb = _bmm_nt(cgs, bgs)
    xd_p = jnp.stack([xd[rs(b), g * gw:(g + 1) * gw] for b, g in pb])
    yd = None
    for j in range(hpg):
        lmat = jnp.stack([jnp.exp(jnp.where(causal, cums[b][:, g * hpg + j:g * hpg + j + 1]
                                            - cum_ts[b][g * hpg + j:g * hpg + j + 1, :], -jnp.inf)) for b, g in pb])
        xm = jnp.where((lane >= SSD_P * j) & (lane < SSD_P * (j + 1)), xd_p, 0.0)
        t = _bmm(cb * lmat, xm)
        yd = t if yd is None else yd + t
    state = s_ref[...]
    cum_p = jnp.stack([cum_e[rs(b), g * gw:(g + 1) * gw] for b, g in pb])
    y = yd + _bmm(cgs, state) * jnp.exp(cum_p)
    dec = jnp.stack([jnp.exp(last_e[b * c:b * c + 1, g * gw:(g + 1) * gw]) for b, g in pb])
    bg_t = jnp.stack([bc_ref[b, :, g * SSD_N:(g + 1) * SSD_N].astype(F32).T for b, g in pb])
    s_ref[...] = state * dec + _bmm(bg_t, jnp.stack([xdw[rs(b), g * gw:(g + 1) * gw] for b, g in pb]))

    for p, (b, g) in enumerate(pb):
        gsl = slice(g * gw, (g + 1) * gw)
        yg = (y[p] + de_ref[:, gsl] * xs[rs(b), gsl]) * _silu(z_ref[b, :, gsl].astype(F32))
        o_ref[b, :, gsl] = (_rms(yg) * nw_ref[:, gsl]).astype(o_ref.dtype)


def _ssd(z, xs, bc, dt, dt_bias, a_log, d_e, norm_w, e8, bsz, seqlen, chunk=256):
    nt = seqlen // chunk
    inner = SSD_HEADS * SSD_P
    row = lambda w: pl.BlockSpec((bsz, chunk, w), lambda i: (0, i, 0))
    seq = lambda a: a.reshape(bsz, seqlen, a.shape[-1])
    return pl.pallas_call(
        functools.partial(_ssd_body, chunk=chunk),
        grid=(nt,),
        in_specs=[row(inner), row(inner), row(bc.shape[1]), row(LANES), _full((1, LANES)), _full((1, LANES)),
                  _full((1, inner)), _full((1, inner)), _full(e8.shape)],
        out_specs=row(inner),
        out_shape=jax.ShapeDtypeStruct((bsz, seqlen, inner), BF16),
        scratch_shapes=[pltpu.VMEM((bsz * SSD_GROUPS, SSD_N, inner // SSD_GROUPS), F32)],
        compiler_params=_params(1),
        name="ssd",
    )(seq(z), seq(xs), seq(bc), seq(dt), dt_bias, a_log, d_e, norm_w, e8).reshape(bsz * seqlen, inner)


def _gdn_body(q_ref, k_ref, v_ref, z_ref, b_ref, a_ref, al_ref, dtb_ref, nw_ref, e6_ref, o_ref, s_ref, *, tile):
    bsz = q_ref.shape[0]
    ck = GDN_CHUNK
    nck = tile // ck
    nall = bsz * nck
    i = pl.program_id(0)
    qk_w = GDN_HEADS * GDN_DK
    rows_of = lambda ref: jnp.concatenate([ref[b] for b in range(bsz)], axis=0)

    @pl.when(i == 0)
    def _():
        s_ref[...] = jnp.zeros_like(s_ref)

    beta = _sigmoid(rows_of(b_ref))
    gl = -jnp.exp(al_ref[...]) * _softplus(rows_of(a_ref) + dtb_ref[...])

    rowt = lax.broadcasted_iota(jnp.int32, (tile, tile), 0)
    colt = lax.broadcasted_iota(jnp.int32, (tile, tile), 1)
    chunk_causal = ((rowt >> 6) == (colt >> 6)) & (rowt >= colt)
    gc = jnp.concatenate([_rows01(chunk_causal, gl[b * tile:(b + 1) * tile]) for b in range(bsz)], axis=0)
    gc_t = gc.T
    e6 = e6_ref[...]
    gc_e = _cols01(gc, e6)
    beta_e = _cols01(beta, e6)
    eg_e = jnp.exp(gc_e)
    glast_rows = [jnp.broadcast_to(gc_e[(cix + 1) * ck - 1:(cix + 1) * ck, :], (ck, qk_w)) for cix in range(nall)]
    glast_e = jnp.concatenate(glast_rows, axis=0)
    tail_e = jnp.exp(glast_e - gc_e)
    dlast_e = jnp.exp(glast_e)

    rowi = lax.broadcasted_iota(jnp.int32, (ck, ck), 0)
    coli = lax.broadcasted_iota(jnp.int32, (ck, ck), 1)
    causal = rowi >= coli
    strict = rowi > coli

    def same_block(bits):
        return (rowi >> bits) == (coli >> bits)

    eye = (rowi == coli).astype(F32)
    base_mask = same_block(3) & strict
    level_masks = [same_block(b + 1) & (~same_block(b)) & strict for b in (3, 4, 5)]

    q_all, k_all, v_all = (rows_of(r).astype(F32) for r in (q_ref, k_ref, v_ref))
    q_l, k_l, kb_l, rhs_l, kt_l, qg_l, decay_l = [], [], [], [], [], [], []
    for h in range(GDN_HEADS):
        sl = slice(h * GDN_DK, (h + 1) * GDN_DK)
        qh = q_all[:, sl]
        kh = k_all[:, sl]
        vh = v_all[:, sl]
        qh = qh * lax.rsqrt(jnp.sum(qh * qh, axis=-1, keepdims=True) + EPS) * (GDN_DK ** -0.5)
        kh = kh * lax.rsqrt(jnp.sum(kh * kh, axis=-1, keepdims=True) + EPS)
        bh = beta_e[:, sl]
        kb = kh * bh
        rhs = jnp.concatenate([vh * bh, kb * eg_e[:, sl]], axis=1)
        kt = kh * tail_e[:, sl]
        qg = qh * eg_e[:, sl]
        for cix in range(nall):
            rs = slice(cix * ck, (cix + 1) * ck)
            q_l.append(qh[rs])
            k_l.append(kh[rs])
            kb_l.append(kb[rs])
            rhs_l.append(rhs[rs])
            kt_l.append(kt[rs])
            qg_l.append(qg[rs])
            seg = gc[rs, h:h + 1] - gc_t[h:h + 1, rs]
            decay_l.append(jnp.exp(jnp.where(causal, seg, -jnp.inf)))
    decay = jnp.stack(decay_l)
    kq = _bmm_nt(jnp.stack([jnp.concatenate([kb, qq], axis=0) for kb, qq in zip(kb_l, q_l)]), jnp.stack(k_l))
    lower = jnp.where(strict, kq[:, :ck] * decay, 0.0)
    attn = kq[:, ck:] * decay

    d8 = jnp.where(base_mask, lower, 0.0)
    x2 = _bmm(d8, d8)
    x4 = _bmm(x2, x2)
    tinv = eye - d8
    tinv = tinv + _bmm(tinv, x2)
    tinv = tinv + _bmm(tinv, x4)
    for lm in level_masks:
        e = jnp.where(lm, lower, 0.0)
        tinv = tinv - _bmm(tinv, _bmm(e, tinv))

    uw = _bmm(tinv, jnp.stack(rhs_l))

    def of_chunk(xs, cix):
        return jnp.stack([xs[h * nall + b * nck + cix] for b in range(bsz) for h in range(GDN_HEADS)])

    state = s_ref[...]
    outs = []
    for cix in range(nck):
        uw_c = of_chunk(uw, cix)
        r = _bmm(jnp.concatenate([uw_c[:, :, GDN_DK:], of_chunk(qg_l, cix)], axis=1), state)
        v_new = uw_c[:, :, :GDN_DK] - r[:, :ck]
        kt_t = jnp.stack([kt_l[h * nall + b * nck + cix].T for b in range(bsz) for h in range(GDN_HEADS)])
        av = _bmm(jnp.concatenate([of_chunk(attn, cix), kt_t], axis=1), v_new)
        outs.append(r[:, ck:] + av[:, :ck])
        dl = jnp.stack([dlast_e[(b * nck + cix) * ck:(b * nck + cix) * ck + 1, h * GDN_DK:(h + 1) * GDN_DK]
                        for b in range(bsz) for h in range(GDN_HEADS)])
        state = state * dl + av[:, ck:]
    s_ref[...] = state
    for b in range(bsz):
        for h in range(GDN_HEADS):
            sl = slice(h * GDN_DK, (h + 1) * GDN_DK)
            o = jnp.concatenate([outs[cix][b * GDN_HEADS + h] for cix in range(nck)], axis=0)
            gate = _silu(z_ref[b, :, sl].astype(F32))
            o_ref[b, :, sl] = (_rms(o) * nw_ref[...] * gate).astype(o_ref.dtype)


def _gdn(q, k, v, z, b_raw, a_raw, a_log, dt_bias, norm_w, e6, bsz, seqlen, tile=256):
    nt = seqlen // tile
    wv = GDN_HEADS * GDN_DK
    row = lambda w: pl.BlockSpec((bsz, tile, w), lambda i: (0, i, 0))
    seq = lambda a: a.reshape(bsz, seqlen, a.shape[-1])
    return pl.pallas_call(
        functools.partial(_gdn_body, tile=tile),
        grid=(nt,),
        in_specs=[row(wv), row(wv), row(wv), row(wv), row(LANES), row(LANES), _full((1, LANES)), _full((1, LANES)),
                  _full((1, GDN_DK)), _full(e6.shape)],
        out_specs=row(wv),
        out_shape=jax.ShapeDtypeStruct((bsz, seqlen, wv), BF16),
        scratch_shapes=[pltpu.VMEM((bsz * GDN_HEADS, GDN_DK, GDN_DK), F32)],
        compiler_params=_params(1),
        name="gdn",
    )(seq(q), seq(k), seq(v), seq(z), seq(b_raw), seq(a_raw), a_log, dt_bias, norm_w, e6).reshape(bsz * seqlen, wv)


def _cmul_add(ar, ai, br, bi, cr, ci):
    return ar * br - ai * bi + cr, ar * bi + ai * br + ci


def _s5_body(u_ref, are_ref, aim_ref, ls_ref, bre_ref, bim_ref, cre_ref, cim_ref, d_ref, wg_ref, bg_ref,
             o_ref, pre_ref, pim_ref, qre_ref, qim_ref, bsr_ref, bsi_ref, h_ref, hre_ref, him_ref, *, tile):
    bsz = u_ref.shape[0]
    tt = tile
    sub = 8
    seg = tt // sub
    seg_bits = seg.bit_length() - 1
    ns = S5_GROUPS * S5_STATE
    i = pl.program_id(0)

    @pl.when(i == 0)
    def _():
        step = jnp.exp(ls_ref[...])

        def powers(n, scale):
            tpos = (lax.broadcasted_iota(jnp.int32, (n, 1), 0).astype(F32) + 1.0) * scale
            mag = jnp.exp(tpos * (are_ref[...] * step))
            ang = tpos * (aim_ref[...] * step)
            return mag * jnp.cos(ang), mag * jnp.sin(ang)

        pre_ref[...], pim_ref[...] = powers(seg, 1.0)
        qre_ref[...], qim_ref[...] = powers(sub, float(seg))
        h_ref[...] = jnp.zeros_like(h_ref)
        a_re, a_im = are_ref[...], aim_ref[...]
        n_re, n_im = pre_ref[0:1, :] - 1.0, pim_ref[0:1, :]
        den = a_re * a_re + a_im * a_im
        c_re = (n_re * a_re + n_im * a_im) / den
        c_im = (n_im * a_re - n_re * a_im) / den
        b_re, b_im = bre_ref[...], bim_ref[...]
        bsr_ref[...] = (c_re * b_re - c_im * b_im).astype(BF16)
        bsi_ref[...] = (c_re * b_im + c_im * b_re).astype(BF16)

    row_i = lax.broadcasted_iota(jnp.int32, (tt, tt), 0)
    col_i = lax.broadcasted_iota(jnp.int32, (tt, tt), 1)
    perm = col_i == (row_i & (sub - 1)) * seg + (row_i >> 3)
    unperm = col_i == (row_i & (seg - 1)) * sub + (row_i >> seg_bits)
    u = jnp.concatenate([_rows01(perm, u_ref[b]) for b in range(bsz)], axis=0)

    hre_ref[...] = _mm(u, bsr_ref[...])
    him_ref[...] = _mm(u, bsi_ref[...])

    sl_i = lax.broadcasted_iota(jnp.int32, (sub, LANES), 0)

    def down(x, d, fill):
        return jnp.where(sl_i >= d, pltpu.roll(x, d, 0), fill)

    for b in range(bsz):
        r0 = b * tt
        for j in range(ns // LANES):
            ls = slice(j * LANES, (j + 1) * LANES)
            ar, ai = pre_ref[0:1, ls], pim_ref[0:1, ls]
            gr, gi = hre_ref[r0:r0 + sub, ls], him_ref[r0:r0 + sub, ls]
            for t in range(1, seg):
                rs = slice(r0 + t * sub, r0 + (t + 1) * sub)
                gr, gi = _cmul_add(ar, ai, gr, gi, hre_ref[rs, ls], him_ref[rs, ls])
                hre_ref[rs, ls] = gr
                him_ref[rs, ls] = gi
            fr, fi = gr, gi
            for k in range(3):
                d = 1 << k
                mr, mi = qre_ref[d - 1:d, ls], qim_ref[d - 1:d, ls]
                fr, fi = _cmul_add(mr, mi, down(fr, d, 0.0), down(fi, d, 0.0), fr, fi)
            cr, ci = h_ref[2 * b:2 * b + 1, ls], h_ref[2 * b + 1:2 * b + 2, ls]
            zr, zi = _cmul_add(qre_ref[:, ls], qim_ref[:, ls], cr, ci, fr, fi)
            h_ref[2 * b:2 * b + 1, ls] = zr[sub - 1:sub, :]
            h_ref[2 * b + 1:2 * b + 2, ls] = zi[sub - 1:sub, :]
            er, ei = down(zr, 1, cr), down(zi, 1, ci)
            for t in range(seg):
                rs = slice(r0 + t * sub, r0 + (t + 1) * sub)
                hr, hi = _cmul_add(pre_ref[t:t + 1, ls], pim_ref[t:t + 1, ls], er, ei,
                                   hre_ref[rs, ls], him_ref[rs, ls])
                hre_ref[rs, ls] = hr
                him_ref[rs, ls] = hi

    y = _mm(hre_ref[...], cre_ref[...]) - _mm(him_ref[...], cim_ref[...]) + d_ref[...] * u
    y = jax.nn.gelu(y)
    out = y * _sigmoid(_mm(y, wg_ref[...]) + bg_ref[...])
    for b in range(bsz):
        o_ref[b] = _mm(unperm, out[b * tt:(b + 1) * tt]).astype(o_ref.dtype)


def _s5(u, a_re, a_im, log_step, b_re, b_im, c_re, c_im, d_skip, w_glu, b_glu, bsz, seqlen, tile=256):
    nt = seqlen // tile
    ns = S5_GROUPS * S5_STATE
    row = pl.BlockSpec((bsz, tile, S5_CH), lambda i: (0, i, 0))
    vec = _full((1, ns))
    carry_rows = HALO * pl.cdiv(2 * bsz, HALO)
    return pl.pallas_call(
        functools.partial(_s5_body, tile=tile),
        grid=(nt,),
        in_specs=[row, vec, vec, vec, _full((S5_CH, ns)), _full((S5_CH, ns)), _full((ns, S5_CH)),
                  _full((ns, S5_CH)), _full((1, S5_CH)), _full((S5_CH, S5_CH)), _full((1, S5_CH))],
        out_specs=row,
        out_shape=jax.ShapeDtypeStruct((bsz, seqlen, S5_CH), BF16),
        scratch_shapes=[pltpu.VMEM((tile // 8, ns), F32), pltpu.VMEM((tile // 8, ns), F32),
                        pltpu.VMEM((8, ns), F32), pltpu.VMEM((8, ns), F32),
                        pltpu.VMEM((S5_CH, ns), BF16), pltpu.VMEM((S5_CH, ns), BF16),
                        pltpu.VMEM((carry_rows, ns), F32),
                        pltpu.VMEM((bsz * tile, ns), F32), pltpu.VMEM((bsz * tile, ns), F32)],
        compiler_params=_params(1),
        name="s5",
    )(u.reshape(bsz, seqlen, S5_CH), a_re, a_im, log_step, b_re, b_im, c_re, c_im, d_skip, w_glu, b_glu
      ).reshape(bsz * seqlen, S5_CH)


def _pad_lanes(v, width=LANES):
    v = v.reshape(1, -1).astype(F32)
    return jnp.pad(v, ((0, 0), (0, width - v.shape[1])))


def _pad_cols(w, width=LANES):
    return jnp.pad(w, ((0, 0), (0, width - w.shape[1])))


def _head_expander(n_heads, head_width, rows=LANES):
    r = jnp.arange(rows)[:, None]
    col_head = jnp.arange(n_heads * head_width)[None, :] // head_width
    return (r == col_head).astype(BF16)


def _block_diag(blocks):
    g, r, c = blocks.shape
    eye = jnp.eye(g, dtype=blocks.dtype)
    return (eye[:, None, :, None] * blocks[:, :, None, :]).reshape(g * r, g * c)


def kernel(x, l0_norm_mix, l0_w_in, ssd_conv_w, ssd_conv_b, ssd_dt_bias, ssd_A_log, ssd_D, ssd_norm_w, l0_w_out, l0_norm_mlp, l0_w_up, l0_w_down, l1_norm_mix, l1_w_in, gdn_conv_w, gdn_A_log, gdn_dt_bias, gdn_norm_w, s5_A_re, s5_A_im, s5_log_step, s5_B_re, s5_B_im, s5_C_re, s5_C_im, s5_D, s5_w_glu, s5_b_glu, l1_w_out, l1_norm_mlp, l1_w_up, l1_w_down, final_norm):
    bsz, seqlen, d = x.shape
    x2d = x.reshape(bsz * seqlen, d)
    row = lambda v: v.reshape(1, -1).astype(F32)
    bf = lambda w: w.astype(BF16)

    ret_w = RET_HEADS * RET_DK
    ssd_inner = SSD_HEADS * SSD_P
    ssd_xbc = ssd_inner + 2 * SSD_GROUPS * SSD_N
    offs, acc = [], 0
    for wdt in (ret_w, ret_w, ret_w, ret_w, ssd_inner, ssd_inner, ssd_xbc - ssd_inner, SSD_HEADS):
        offs.append((acc, acc + wdt))
        acc += wdt
    wq, wk, wv, wgate, wz, wxs, wbc, wdt_ = [l0_w_in[:, a:b] for a, b in offs]
    half = RET_DK // 2
    inv = ROPE_THETA ** (-jnp.arange(half, dtype=F32) / half)
    inv2 = jnp.concatenate([inv, inv]).reshape(1, RET_DK)
    sgn = jnp.concatenate([-jnp.ones((half,), F32), jnp.ones((half,), F32)]).reshape(1, RET_DK)
    xs, bc, q, k, v, gate, z, dt = _norm_proj(
        x2d, row(l0_norm_mix), [bf(w) for w in (wxs, wbc, wq, wk, wv, wgate, wz, _pad_cols(wdt_))],
        ("conv_bias", "conv_bias", "rope_q", "rope_k", "plain", "plain", "plain", "plain"), seqlen,
        conv_w=ssd_conv_w.astype(F32), conv_b=row(ssd_conv_b), rope=(inv2, sgn),
        out_dtypes=(F32, BF16, BF16, BF16, BF16, BF16, BF16, F32))

    ret_out = _retention(q, k, v, gate, bsz, seqlen)
    ssd_out = _ssd(z, xs, bc, dt, _pad_lanes(ssd_dt_bias), _pad_lanes(ssd_A_log),
                   row(jnp.repeat(ssd_D.astype(F32), SSD_P)), row(ssd_norm_w),
                   _head_expander(SSD_HEADS, SSD_P), bsz, seqlen)

    x2d = _out_mlp(x2d, [ret_out, ssd_out], [bf(l0_w_out[:ret_w]), bf(l0_w_out[ret_w:])],
                   row(l0_norm_mlp), bf(l0_w_up), bf(l0_w_down))

    gdn_w = GDN_HEADS * GDN_DK
    offs, acc = [], 0
    for wdt in (gdn_w, gdn_w, gdn_w, gdn_w, GDN_HEADS, GDN_HEADS, S5_CH):
        offs.append((acc, acc + wdt))
        acc += wdt
    w1 = [l1_w_in[:, a:b] for a, b in offs]
    w1[4] = _pad_cols(w1[4])
    w1[5] = _pad_cols(w1[5])
    qg, kg, vg, zg, b_raw, a_raw, u = _norm_proj(
        x2d, row(l1_norm_mix), [bf(w) for w in w1],
        ("conv", "conv", "conv", "plain", "plain", "plain", "plain"), seqlen, conv_w=gdn_conv_w.astype(F32),
        out_dtypes=(BF16, BF16, BF16, BF16, F32, F32, F32))

    gdn_out = _gdn(qg, kg, vg, zg, b_raw, a_raw, _pad_lanes(gdn_A_log), _pad_lanes(gdn_dt_bias), row(gdn_norm_w),
                   _head_expander(GDN_HEADS, GDN_DK), bsz, seqlen)

    flat = lambda p: p.reshape(1, S5_GROUPS * S5_STATE).astype(F32)
    b_re = _block_diag(jnp.swapaxes(s5_B_re.astype(F32), 1, 2))
    b_im = _block_diag(jnp.swapaxes(s5_B_im.astype(F32), 1, 2))
    c_re = _block_diag(jnp.swapaxes(s5_C_re.astype(F32), 1, 2))
    c_im = _block_diag(jnp.swapaxes(s5_C_im.astype(F32), 1, 2))
    s5_out = _s5(u, flat(s5_A_re), flat(s5_A_im), row(jnp.repeat(s5_log_step.astype(F32), S5_STATE)),
                 b_re, b_im, bf(c_re), bf(c_im), row(s5_D), bf(s5_w_glu), row(s5_b_glu), bsz, seqlen)

    out = _out_mlp(x2d, [gdn_out, s5_out], [bf(l1_w_out[:gdn_w]), bf(l1_w_out[gdn_w:])],
                   row(l1_norm_mlp), bf(l1_w_up), bf(l1_w_down), final_gain=row(final_norm))
    return out.reshape(bsz, seqlen, d)
```

```python
import functools
import math

import jax
import jax.numpy as jnp
from jax import lax
from jax.experimental import pallas as pl
from jax.experimental.pallas import tpu as pltpu

F32 = jnp.float32
BF16 = jnp.bfloat16
EPS = 1e-6
CONV_WIDTH = 4
HALO = 8

RET_HEADS, RET_DK = 4, 128
SSD_HEADS, SSD_P, SSD_GROUPS, SSD_N = 8, 64, 2, 128
GDN_HEADS, GDN_DK = 6, 128
GDN_CHUNK = 64
S5_CH, S5_GROUPS, S5_GROUP, S5_STATE = 256, 16, 16, 64
ROPE_THETA = 10000.0
LANES = 128

VMEM_LIMIT = 56 * 1024 * 1024


def _full(shape):
    nd = len(shape)
    return pl.BlockSpec(shape, lambda *_: (0,) * nd)


def _resident(shape):
    nd = len(shape)
    return pl.BlockSpec(shape, lambda *_: (0,) * nd, pipeline_mode=pl.Buffered(1))


def _params(n_axes):
    return pltpu.CompilerParams(dimension_semantics=("arbitrary",) * n_axes, vmem_limit_bytes=VMEM_LIMIT)


def _mm(a, b):
    return jnp.dot(a.astype(BF16), b.astype(BF16), preferred_element_type=F32)


def _mm_nt(a, b):
    return lax.dot_general(a.astype(BF16), b.astype(BF16), (((1,), (1,)), ((), ())), preferred_element_type=F32)


def _bmm(a, b):
    return lax.dot_general(a.astype(BF16), b.astype(BF16), (((2,), (1,)), ((0,), (0,))), preferred_element_type=F32)


def _bmm_nt(a, b):
    return lax.dot_general(a.astype(BF16), b.astype(BF16), (((2,), (2,)), ((0,), (0,))), preferred_element_type=F32)


def _split3(a):
    hi = a.astype(BF16)
    r1 = a - hi.astype(F32)
    mid = r1.astype(BF16)
    lo = (r1 - mid.astype(F32)).astype(BF16)
    return hi, mid, lo


def _rows01(m01, a):
    n = a.shape[1]
    p = jnp.dot(m01.astype(BF16), jnp.concatenate(_split3(a), axis=1), preferred_element_type=F32)
    return p[:, :n] + p[:, n:2 * n] + p[:, 2 * n:]


def _cols01(a, e01):
    hi, mid, lo = _split3(a)
    dot = lambda x: jnp.dot(x, e01, preferred_element_type=F32)
    return dot(hi) + dot(mid) + dot(lo)


def _sigmoid(x):
    return 1.0 / (1.0 + jnp.exp(-x))


def _silu(x):
    return x * _sigmoid(x)


def _softplus(x):
    return jnp.maximum(x, 0.0) + jnp.log(1.0 + jnp.exp(-jnp.abs(x)))


def _rms(x):
    return x * lax.rsqrt(jnp.mean(x * x, axis=-1, keepdims=True) + EPS)


def _norm_proj_body(*refs, kinds, tiles_per_seq, sub_rows):
    it = iter(refs)
    x_ref, g_ref = next(it), next(it)
    w_refs = [next(it) for _ in kinds]
    has_conv = any(k.startswith("conv") for k in kinds)
    has_rope = any(k.startswith("rope") for k in kinds)
    cw_ref = next(it) if has_conv else None
    cb_ref = next(it) if "conv_bias" in kinds else None
    inv_ref, sgn_ref = (next(it), next(it)) if has_rope else (None, None)
    o_refs = [next(it) for _ in kinds]
    xe_ref = next(it) if has_conv else None
    rc_ref, rs_ref = (next(it), next(it)) if has_rope else (None, None)

    tm = x_ref.shape[0]
    t_in_seq = pl.program_id(0) % tiles_per_seq

    if has_conv:
        @pl.when(t_in_seq == 0)
        def _():
            xe_ref[:, 0:HALO, :] = jnp.zeros((xe_ref.shape[0], HALO, LANES), F32)

    if has_rope:
        @pl.when(pl.program_id(0) == 0)
        def _():
            ang_r = lax.broadcasted_iota(jnp.int32, (tm, 1), 0).astype(F32) * inv_ref[...]
            rc_ref[...] = jnp.cos(ang_r)
            rs_ref[...] = jnp.sin(ang_r)

        ang_0 = (t_in_seq * tm).astype(F32) * inv_ref[...]
        c0, s0 = jnp.cos(ang_0), jnp.sin(ang_0)
        cos = rc_ref[...] * c0 - rs_ref[...] * s0
        sin = (rs_ref[...] * c0 + rc_ref[...] * s0) * sgn_ref[...]

    subs = list(range(0, tm, sub_rows))
    xbs = [(_rms(x_ref[r0:r0 + sub_rows, :]) * g_ref[...]).astype(BF16) for r0 in subs]
    conv_col = 0
    for kind, w_ref, o_ref in zip(kinds, w_refs, o_refs):
        width = w_ref.shape[1]
        ys = [jnp.dot(xb, w_ref[...], preferred_element_type=F32) for xb in xbs]
        for r0, y in zip(subs, ys):
            rs = slice(r0, r0 + sub_rows)
            if kind.startswith("rope"):
                scale = RET_DK ** -0.5 if kind == "rope_q" else 1.0
                for h in range(width // RET_DK):
                    sl = slice(h * RET_DK, (h + 1) * RET_DK)
                    yh = y[:, sl]
                    rot = yh * cos[rs] + pltpu.roll(yh, RET_DK // 2, 1) * sin[rs]
                    o_ref[rs, sl] = (rot * scale).astype(o_ref.dtype)
            elif kind.startswith("conv"):
                for lt in range(width // LANES):
                    ct = conv_col // LANES + lt
                    ls = slice(conv_col + lt * LANES, conv_col + (lt + 1) * LANES)
                    xe_ref[ct, HALO + r0:HALO + r0 + sub_rows, :] = y[:, lt * LANES:(lt + 1) * LANES]
                    acc = None
                    for j in range(CONV_WIDTH):
                        off = HALO - (CONV_WIDTH - 1) + j + r0
                        term = cw_ref[j:j + 1, ls] * xe_ref[ct, off:off + sub_rows, :]
                        acc = term if acc is None else acc + term
                    if kind == "conv_bias":
                        acc = acc + cb_ref[:, ls]
                    o_ref[rs, lt * LANES:(lt + 1) * LANES] = _silu(acc).astype(o_ref.dtype)
            else:
                o_ref[rs, :] = y.astype(o_ref.dtype)
        if kind.startswith("conv"):
            for lt in range(width // LANES):
                ct = conv_col // LANES + lt
                xe_ref[ct, 0:HALO, :] = xe_ref[ct, tm:tm + HALO, :]
            conv_col += width


def _norm_proj(x2d, gain, weights, kinds, seqlen, conv_w=None, conv_b=None, rope=None, out_dtypes=None, tm=512,
               sub_rows=128):
    t, d = x2d.shape
    out_dtypes = out_dtypes or (F32,) * len(weights)
    in_specs = [pl.BlockSpec((tm, d), lambda i: (i, 0)), _full((1, d))] + [_resident(w.shape) for w in weights]
    args = [x2d, gain, *weights]
    scratch = []
    if conv_w is not None:
        in_specs.append(_full(conv_w.shape))
        args.append(conv_w)
        if conv_b is not None:
            in_specs.append(_full(conv_b.shape))
            args.append(conv_b)
        scratch.append(pltpu.VMEM((conv_w.shape[1] // LANES, tm + HALO, LANES), F32))
    if rope is not None:
        in_specs += [_full(r.shape) for r in rope]
        args += list(rope)
        scratch += [pltpu.VMEM((tm, RET_DK), F32), pltpu.VMEM((tm, RET_DK), F32)]
    return pl.pallas_call(
        functools.partial(_norm_proj_body, kinds=tuple(kinds), tiles_per_seq=seqlen // tm, sub_rows=sub_rows),
        grid=(t // tm,),
        in_specs=in_specs,
        out_specs=[pl.BlockSpec((tm, w.shape[1]), lambda i: (i, 0)) for w in weights],
        out_shape=[jax.ShapeDtypeStruct((t, w.shape[1]), dt) for w, dt in zip(weights, out_dtypes)],
        scratch_shapes=scratch,
        compiler_params=_params(1),
        name="norm_proj",
    )(*args)


def _out_mlp_body(x_ref, *refs, n_mix, ff_chunk, final):
    mix_refs, wo_refs = refs[:n_mix], refs[n_mix:2 * n_mix]
    rest = refs[2 * n_mix:]
    if final:
        g_ref, wu_ref, wd_ref, gf_ref, o_ref = rest
    else:
        g_ref, wu_ref, wd_ref, o_ref = rest
    d_ff = wu_ref.shape[1]
    x1 = x_ref[...]
    for m_ref, w_ref in zip(mix_refs, wo_refs):
        x1 = x1 + jnp.dot(m_ref[...].astype(BF16), w_ref[...], preferred_element_type=F32)
    xb = (_rms(x1) * g_ref[...]).astype(BF16)
    mlp = None
    for c in range(d_ff // ff_chunk):
        a = jnp.dot(xb, wu_ref[:, c * ff_chunk:(c + 1) * ff_chunk], preferred_element_type=F32)
        a = jnp.maximum(a, 0.0)
        t = jnp.dot((a * a).astype(BF16), wd_ref[c * ff_chunk:(c + 1) * ff_chunk, :],
                    preferred_element_type=F32)
        mlp = t if mlp is None else mlp + t
    acc = x1 + mlp
    if final:
        acc = _rms(acc) * gf_ref[...]
    o_ref[...] = acc


def _out_mlp(x2d, mixes, w_outs, gain, w_up, w_down, final_gain=None, tm=512, ff_chunk=1024):
    t, d = x2d.shape
    n_mix = len(mixes)
    final = final_gain is not None
    row = lambda w: pl.BlockSpec((tm, w), lambda i: (i, 0))
    in_specs = [row(d)] + [row(m.shape[1]) for m in mixes] + [_resident(w.shape) for w in w_outs]
    in_specs += [_full((1, d)), _resident(w_up.shape), _resident(w_down.shape)]
    args = [x2d, *mixes, *w_outs, gain, w_up, w_down]
    if final:
        in_specs.append(_full((1, d)))
        args.append(final_gain)
    return pl.pallas_call(
        functools.partial(_out_mlp_body, n_mix=n_mix, ff_chunk=ff_chunk, final=final),
        grid=(t // tm,),
        in_specs=in_specs,
        out_specs=row(d),
        out_shape=jax.ShapeDtypeStruct((t, d), F32),
        compiler_params=_params(1),
        name="out_mlp",
    )(*args)


def _retention_body(q_ref, k_ref, v_ref, g_ref, o_ref, s_ref, dm_ref, *, chunk):
    bsz = q_ref.shape[0]
    c = chunk
    i = pl.program_id(0)
    gammas = [math.log(1.0 - 2.0 ** (-5.0 - h)) for h in range(RET_HEADS)]

    @pl.when(i == 0)
    def _():
        s_ref[...] = jnp.zeros_like(s_ref)
        rowi = lax.broadcasted_iota(jnp.int32, (c, c), 0)
        coli = lax.broadcasted_iota(jnp.int32, (c, c), 1)
        diff = (rowi - coli).astype(F32)
        for h in range(RET_HEADS):
            dm_ref[h] = jnp.where(rowi >= coli, jnp.exp(gammas[h] * diff), 0.0)

    pbh = [(b, h) for b in range(bsz) for h in range(RET_HEADS)]
    head = lambda ref, b, h: ref[b, :, h * RET_DK:(h + 1) * RET_DK]
    idx = lax.broadcasted_iota(jnp.int32, (c, 1), 0).astype(F32)
    q8 = jnp.stack([head(q_ref, b, h) for b, h in pbh])
    k8 = jnp.stack([head(k_ref, b, h) for b, h in pbh])
    v8 = jnp.stack([head(v_ref, b, h) for b, h in pbh])
    scores = _bmm_nt(q8, k8) * jnp.stack([dm_ref[h] for _, h in pbh])
    q_dec = jnp.stack([jnp.exp(gammas[h] * (idx + 1.0)) for _, h in pbh])
    state = s_ref[...]
    y = _bmm(scores, v8) + _bmm(q8 * q_dec, state)
    kw_t = jnp.stack([(head(k_ref, b, h) * jnp.exp(gammas[h] * (c - 1.0 - idx))).T for b, h in pbh])
    chunk_decay = jnp.stack([jnp.full((1, RET_DK), math.exp(gammas[h] * c), F32) for _, h in pbh])
    s_ref[...] = state * chunk_decay + _bmm(kw_t, v8)
    for p, (b, h) in enumerate(pbh):
        gt = head(g_ref, b, h).astype(F32)
        o_ref[b, :, h * RET_DK:(h + 1) * RET_DK] = (_silu(gt) * _rms(y[p])).astype(o_ref.dtype)


def _retention(q, k, v, gate, bsz, seqlen, chunk=256):
    nt = seqlen // chunk
    w = RET_HEADS * RET_DK
    row = pl.BlockSpec((bsz, chunk, w), lambda i: (0, i, 0))
    seq = lambda a: a.reshape(bsz, seqlen, w)
    return pl.pallas_call(
        functools.partial(_retention_body, chunk=chunk),
        grid=(nt,),
        in_specs=[row, row, row, row],
        out_specs=row,
        out_shape=jax.ShapeDtypeStruct((bsz, seqlen, w), BF16),
        scratch_shapes=[pltpu.VMEM((bsz * RET_HEADS, RET_DK, RET_DK), F32),
                        pltpu.VMEM((RET_HEADS, chunk, chunk), F32)],
        compiler_params=_params(1),
        name="retention",
    )(seq(q), seq(k), seq(v), seq(gate)).reshape(bsz * seqlen, w)


def _ssd_body(z_ref, xs_ref, bc_ref, dt_ref, dtb_ref, al_ref, de_ref, nw_ref, e8_ref, o_ref, s_ref, *, chunk):
    bsz = xs_ref.shape[0]
    c = chunk
    i = pl.program_id(0)
    inner = SSD_HEADS * SSD_P
    gw = inner // SSD_GROUPS
    hpg = SSD_HEADS // SSD_GROUPS
    rows_of = lambda ref: jnp.concatenate([ref[b] for b in range(bsz)], axis=0)

    @pl.when(i == 0)
    def _():
        s_ref[...] = jnp.zeros_like(s_ref)

    xs = rows_of(xs_ref)
    dtv = _softplus(rows_of(dt_ref) + dtb_ref[...])
    la = dtv * (-jnp.exp(al_ref[...]))
    rowi = lax.broadcasted_iota(jnp.int32, (c, c), 0)
    coli = lax.broadcasted_iota(jnp.int32, (c, c), 1)
    causal = rowi >= coli
    cums = [_rows01(causal, la[b * c:(b + 1) * c]) for b in range(bsz)]
    cum_ts = [cm_.T for cm_ in cums]
    e8 = e8_ref[...]
    dt_e = _cols01(dtv, e8)
    cum_e = _cols01(jnp.concatenate(cums, axis=0), e8)
    last_e = jnp.concatenate([jnp.broadcast_to(cum_e[(b + 1) * c - 1:(b + 1) * c, :], (c, inner)) for b in range(bsz)],
                             axis=0)
    xd = xs * dt_e
    xdw = xd * jnp.exp(last_e - cum_e)
    lane = lax.broadcasted_iota(jnp.int32, (1, gw), 1)

    pb = [(b, g) for b in range(bsz) for g in range(SSD_GROUPS)]
    rs = lambda b: slice(b * c, (b + 1) * c)
    c_off = SSD_GROUPS * SSD_N
    bgs = jnp.stack([bc_ref[b, :, g * SSD_N:(g + 1) * SSD_N] for b, g in pb])
    cgs = jnp.stack([bc_ref[b, :, c_off + g * SSD_N:c_off + (g + 1) * SSD_N] for b, g in pb])
    cb = _bmm_nt(cgs, bgs)
    xd_p = jnp.stack([xd[rs(b), g * gw:(g + 1) * gw] for b, g in pb])
    yd = None
    for j in range(hpg):
        lmat = jnp.stack([jnp.exp(jnp.where(causal, cums[b][:, g * hpg + j:g * hpg + j + 1]
                                            - cum_ts[b][g * hpg + j:g * hpg + j + 1, :], -jnp.inf)) for b, g in pb])
        xm = jnp.where((lane >= SSD_P * j) & (lane < SSD_P * (j + 1)), xd_p, 0.0)
        t = _bmm(cb * lmat, xm)
        yd = t if yd is None else yd + t
    state = s_ref[...]
    cum_p = jnp.stack([cum_e[rs(b), g * gw:(g + 1) * gw] for b, g in pb])
    y = yd + _bmm(cgs, state) * jnp.exp(cum_p)
    dec = jnp.stack([jnp.exp(last_e[b * c:b * c + 1, g * gw:(g + 1) * gw]) for b, g in pb])
    bg_t = jnp.stack([bc_ref[b, :, g * SSD_N:(g + 1) * SSD_N].astype(F32).T for b, g in pb])
    s_ref[...] = state * dec + _bmm(bg_t, jnp.stack([xdw[rs(b), g * gw:(g + 1) * gw] for b, g in pb]))

    for p, (b, g) in enumerate(pb):
        gsl = slice(g * gw, (g + 1) * gw)
        yg = (y[p] + de_ref[:, gsl] * xs[rs(b), gsl]) * _silu(z_ref[b, :, gsl].astype(F32))
        o_ref[b, :, gsl] = (_rms(yg) * nw_ref[:, gsl]).astype(o_ref.dtype)


def _ssd(z, xs, bc, dt, dt_bias, a_log, d_e, norm_w, e8, bsz, seqlen, chunk=256):
    nt = seqlen // chunk
    inner = SSD_HEADS * SSD_P
    row = lambda w: pl.BlockSpec((bsz, chunk, w), lambda i: (0, i, 0))
    seq = lambda a: a.reshape(bsz, seqlen, a.shape[-1])
    return pl.pallas_call(
        functools.partial(_ssd_body, chunk=chunk),
        grid=(nt,),
        in_specs=[row(inner), row(inner), row(bc.shape[1]), row(LANES), _full((1, LANES)), _full((1, LANES)),
                  _full((1, inner)), _full((1, inner)), _full(e8.shape)],
        out_specs=row(inner),
        out_shape=jax.ShapeDtypeStruct((bsz, seqlen, inner), BF16),
        scratch_shapes=[pltpu.VMEM((bsz * SSD_GROUPS, SSD_N, inner // SSD_GROUPS), F32)],
        compiler_params=_params(1),
        name="ssd",
    )(seq(z), seq(xs), seq(bc), seq(dt), dt_bias, a_log, d_e, norm_w, e8).reshape(bsz * seqlen, inner)


def _gdn_body(q_ref, k_ref, v_ref, z_ref, b_ref, a_ref, al_ref, dtb_ref, nw_ref, e6_ref, o_ref, s_ref, *, tile):
    bsz = q_ref.shape[0]
    ck = GDN_CHUNK
    nck = tile // ck
    nall = bsz * nck
    i = pl.program_id(0)
    qk_w = GDN_HEADS * GDN_DK
    rows_of = lambda ref: jnp.concatenate([ref[b] for b in range(bsz)], axis=0)

    @pl.when(i == 0)
    def _():
        s_ref[...] = jnp.zeros_like(s_ref)

    beta = _sigmoid(rows_of(b_ref))
    gl = -jnp.exp(al_ref[...]) * _softplus(rows_of(a_ref) + dtb_ref[...])

    rowt = lax.broadcasted_iota(jnp.int32, (tile, tile), 0)
    colt = lax.broadcasted_iota(jnp.int32, (tile, tile), 1)
    chunk_causal = ((rowt >> 6) == (colt >> 6)) & (rowt >= colt)
    gc = jnp.concatenate([_rows01(chunk_causal, gl[b * tile:(b + 1) * tile]) for b in range(bsz)], axis=0)
    gc_t = gc.T
    e6 = e6_ref[...]
    gc_e = _cols01(gc, e6)
    beta_e = _cols01(beta, e6)
    eg_e = jnp.exp(gc_e)
    glast_rows = [jnp.broadcast_to(gc_e[(cix + 1) * ck - 1:(cix + 1) * ck, :], (ck, qk_w)) for cix in range(nall)]
    glast_e = jnp.concatenate(glast_rows, axis=0)
    tail_e = jnp.exp(glast_e - gc_e)
    dlast_e = jnp.exp(glast_e)

    rowi = lax.broadcasted_iota(jnp.int32, (ck, ck), 0)
    coli = lax.broadcasted_iota(jnp.int32, (ck, ck), 1)
    causal = rowi >= coli
    strict = rowi > coli

    def same_block(bits):
        return (rowi >> bits) == (coli >> bits)

    eye = (rowi == coli).astype(F32)
    base_mask = same_block(3) & strict
    level_masks = [same_block(b + 1) & (~same_block(b)) & strict for b in (3, 4, 5)]

    q_all, k_all, v_all = rows_of(q_ref), rows_of(k_ref), rows_of(v_ref)
    q_l, k_l, kb_l, rhs_l, kt_l, qg_l, decay_l = [], [], [], [], [], [], []
    for h in range(GDN_HEADS):
        sl = slice(h * GDN_DK, (h + 1) * GDN_DK)
        qh = q_all[:, sl]
        kh = k_all[:, sl]
        vh = v_all[:, sl]
        qh = qh * lax.rsqrt(jnp.sum(qh * qh, axis=-1, keepdims=True) + EPS) * (GDN_DK ** -0.5)
        kh = kh * lax.rsqrt(jnp.sum(kh * kh, axis=-1, keepdims=True) + EPS)
        bh = beta_e[:, sl]
        kb = kh * bh
        rhs = jnp.concatenate([vh * bh, kb * eg_e[:, sl]], axis=1)
        kt = kh * tail_e[:, sl]
        qg = qh * eg_e[:, sl]
        for cix in range(nall):
            rs = slice(cix * ck, (cix + 1) * ck)
            q_l.append(qh[rs])
            k_l.append(kh[rs])
            kb_l.append(kb[rs])
            rhs_l.append(rhs[rs])
            kt_l.append(kt[rs])
            qg_l.append(qg[rs])
            seg = gc[rs, h:h + 1] - gc_t[h:h + 1, rs]
            decay_l.append(jnp.exp(jnp.where(causal, seg, -jnp.inf)))
    decay = jnp.stack(decay_l)
    kq = _bmm_nt(jnp.stack([jnp.concatenate([kb, qq], axis=0) for kb, qq in zip(kb_l, q_l)]), jnp.stack(k_l))
    lower = jnp.where(strict, kq[:, :ck] * decay, 0.0)
    attn = kq[:, ck:] * decay

    d8 = jnp.where(base_mask, lower, 0.0)
    x2 = _bmm(d8, d8)
    x4 = _bmm(x2, x2)
    tinv = eye - d8
    tinv = tinv + _bmm(tinv, x2)
    tinv = tinv + _bmm(tinv, x4)
    for lm in level_masks:
        e = jnp.where(lm, lower, 0.0)
        tinv = tinv - _bmm(tinv, _bmm(e, tinv))

    uw = _bmm(tinv, jnp.stack(rhs_l))

    def of_chunk(xs, cix):
        return jnp.stack([xs[h * nall + b * nck + cix] for b in range(bsz) for h in range(GDN_HEADS)])

    state = s_ref[...]
    outs = []
    for cix in range(nck):
        uw_c = of_chunk(uw, cix)
        r = _bmm(jnp.concatenate([uw_c[:, :, GDN_DK:], of_chunk(qg_l, cix)], axis=1), state)
        v_new = uw_c[:, :, :GDN_DK] - r[:, :ck]
        kt_t = jnp.stack([kt_l[h * nall + b * nck + cix].T for b in range(bsz) for h in range(GDN_HEADS)])
        av = _bmm(jnp.concatenate([of_chunk(attn, cix), kt_t], axis=1), v_new)
        outs.append(r[:, ck:] + av[:, :ck])
        dl = jnp.stack([dlast_e[(b * nck + cix) * ck:(b * nck + cix) * ck + 1, h * GDN_DK:(h + 1) * GDN_DK]
                        for b in range(bsz) for h in range(GDN_HEADS)])
        state = state * dl + av[:, ck:]
    s_ref[...] = state
    for b in range(bsz):
        for h in range(GDN_HEADS):
            sl = slice(h * GDN_DK, (h + 1) * GDN_DK)
            o = jnp.concatenate([outs[cix][b * GDN_HEADS + h] for cix in range(nck)], axis=0)
            o_ref[b, :, sl] = (_rms(o) * nw_ref[...] * _silu(z_ref[b, :, sl])).astype(o_ref.dtype)


def _gdn(q, k, v, z, b_raw, a_raw, a_log, dt_bias, norm_w, e6, bsz, seqlen, tile=256):
    nt = seqlen // tile
    wv = GDN_HEADS * GDN_DK
    row = lambda w: pl.BlockSpec((bsz, tile, w), lambda i: (0, i, 0))
    seq = lambda a: a.reshape(bsz, seqlen, a.shape[-1])
    return pl.pallas_call(
        functools.partial(_gdn_body, tile=tile),
        grid=(nt,),
        in_specs=[row(wv), row(wv), row(wv), row(wv), row(LANES), row(LANES), _full((1, LANES)), _full((1, LANES)),
                  _full((1, GDN_DK)), _full(e6.shape)],
        out_specs=row(wv),
        out_shape=jax.ShapeDtypeStruct((bsz, seqlen, wv), BF16),
        scratch_shapes=[pltpu.VMEM((bsz * GDN_HEADS, GDN_DK, GDN_DK), F32)],
        compiler_params=_params(1),
        name="gdn",
    )(seq(q), seq(k), seq(v), seq(z), seq(b_raw), seq(a_raw), a_log, dt_bias, norm_w, e6).reshape(bsz * seqlen, wv)


def _cmul_add(ar, ai, br, bi, cr, ci):
    return ar * br - ai * bi + cr, ar * bi + ai * br + ci


def _s5_body(u_ref, are_ref, aim_ref, ls_ref, bre_ref, bim_ref, cre_ref, cim_ref, d_ref, wg_ref, bg_ref,
             o_ref, pre_ref, pim_ref, qre_ref, qim_ref, bsr_ref, bsi_ref, h_ref, hre_ref, him_ref, *, tile):
    bsz = u_ref.shape[0]
    tt = tile
    sub = 8
    seg = tt // sub
    seg_bits = seg.bit_length() - 1
    ns = S5_GROUPS * S5_STATE
    i = pl.program_id(0)

    @pl.when(i == 0)
    def _():
        step = jnp.exp(ls_ref[...])

        def powers(n, scale):
            tpos = (lax.broadcasted_iota(jnp.int32, (n, 1), 0).astype(F32) + 1.0) * scale
            mag = jnp.exp(tpos * (are_ref[...] * step))
            ang = tpos * (aim_ref[...] * step)
            return mag * jnp.cos(ang), mag * jnp.sin(ang)

        pre_ref[...], pim_ref[...] = powers(seg, 1.0)
        qre_ref[...], qim_ref[...] = powers(sub, float(seg))
        h_ref[...] = jnp.zeros_like(h_ref)
        a_re, a_im = are_ref[...], aim_ref[...]
        n_re, n_im = pre_ref[0:1, :] - 1.0, pim_ref[0:1, :]
        den = a_re * a_re + a_im * a_im
        c_re = (n_re * a_re + n_im * a_im) / den
        c_im = (n_im * a_re - n_re * a_im) / den
        b_re, b_im = bre_ref[...], bim_ref[...]
        bsr_ref[...] = (c_re * b_re - c_im * b_im).astype(BF16)
        bsi_ref[...] = (c_re * b_im + c_im * b_re).astype(BF16)

    row_i = lax.broadcasted_iota(jnp.int32, (tt, tt), 0)
    col_i = lax.broadcasted_iota(jnp.int32, (tt, tt), 1)
    perm = col_i == (row_i & (sub - 1)) * seg + (row_i >> 3)
    unperm = col_i == (row_i & (seg - 1)) * sub + (row_i >> seg_bits)
    u = jnp.concatenate([_rows01(perm, u_ref[b]) for b in range(bsz)], axis=0)

    hre_ref[...] = _mm(u, bsr_ref[...])
    him_ref[...] = _mm(u, bsi_ref[...])

    sl_i = lax.broadcasted_iota(jnp.int32, (sub, LANES), 0)

    def down(x, d, fill):
        return jnp.where(sl_i >= d, pltpu.roll(x, d, 0), fill)

    for b in range(bsz):
        r0 = b * tt
        for j in range(ns // LANES):
            ls = slice(j * LANES, (j + 1) * LANES)
            ar, ai = pre_ref[0:1, ls], pim_ref[0:1, ls]
            gr, gi = hre_ref[r0:r0 + sub, ls], him_ref[r0:r0 + sub, ls]
            for t in range(1, seg):
                rs = slice(r0 + t * sub, r0 + (t + 1) * sub)
                gr, gi = _cmul_add(ar, ai, gr, gi, hre_ref[rs, ls], him_ref[rs, ls])
                hre_ref[rs, ls] = gr
                him_ref[rs, ls] = gi
            fr, fi = gr, gi
            for k in range(3):
                d = 1 << k
                mr, mi = qre_ref[d - 1:d, ls], qim_ref[d - 1:d, ls]
                fr, fi = _cmul_add(mr, mi, down(fr, d, 0.0), down(fi, d, 0.0), fr, fi)
            cr, ci = h_ref[2 * b:2 * b + 1, ls], h_ref[2 * b + 1:2 * b + 2, ls]
            zr, zi = _cmul_add(qre_ref[:, ls], qim_ref[:, ls], cr, ci, fr, fi)
            h_ref[2 * b:2 * b + 1, ls] = zr[sub - 1:sub, :]
            h_ref[2 * b + 1:2 * b + 2, ls] = zi[sub - 1:sub, :]
            er, ei = down(zr, 1, cr), down(zi, 1, ci)
            for t in range(seg):
                rs = slice(r0 + t * sub, r0 + (t + 1) * sub)
                hr, hi = _cmul_add(pre_ref[t:t + 1, ls], pim_ref[t:t + 1, ls], er, ei,
                                   hre_ref[rs, ls], him_ref[rs, ls])
                hre_ref[rs, ls] = hr
                him_ref[rs, ls] = hi

    y = _mm(hre_ref[...], cre_ref[...]) - _mm(him_ref[...], cim_ref[...]) + d_ref[...] * u
    y = jax.nn.gelu(y)
    out = y * _sigmoid(_mm(y, wg_ref[...]) + bg_ref[...])
    for b in range(bsz):
        o_ref[b] = _mm(unperm, out[b * tt:(b + 1) * tt]).astype(o_ref.dtype)


def _s5(u, a_re, a_im, log_step, b_re, b_im, c_re, c_im, d_skip, w_glu, b_glu, bsz, seqlen, tile=256):
    nt = seqlen // tile
    ns = S5_GROUPS * S5_STATE
    row = pl.BlockSpec((bsz, tile, S5_CH), lambda i: (0, i, 0))
    vec = _full((1, ns))
    carry_rows = HALO * pl.cdiv(2 * bsz, HALO)
    return pl.pallas_call(
        functools.partial(_s5_body, tile=tile),
        grid=(nt,),
        in_specs=[row, vec, vec, vec, _full((S5_CH, ns)), _full((S5_CH, ns)), _full((ns, S5_CH)),
                  _full((ns, S5_CH)), _full((1, S5_CH)), _full((S5_CH, S5_CH)), _full((1, S5_CH))],
        out_specs=row,
        out_shape=jax.ShapeDtypeStruct((bsz, seqlen, S5_CH), BF16),
        scratch_shapes=[pltpu.VMEM((tile // 8, ns), F32), pltpu.VMEM((tile // 8, ns), F32),
                        pltpu.VMEM((8, ns), F32), pltpu.VMEM((8, ns), F32),
                        pltpu.VMEM((S5_CH, ns), BF16), pltpu.VMEM((S5_CH, ns), BF16),
                        pltpu.VMEM((carry_rows, ns), F32),
                        pltpu.VMEM((bsz * tile, ns), F32), pltpu.VMEM((bsz * tile, ns), F32)],
        compiler_params=_params(1),
        name="s5",
    )(u.reshape(bsz, seqlen, S5_CH), a_re, a_im, log_step, b_re, b_im, c_re, c_im, d_skip, w_glu, b_glu
      ).reshape(bsz * seqlen, S5_CH)


def _pad_lanes(v, width=LANES):
    v = v.reshape(1, -1).astype(F32)
    return jnp.pad(v, ((0, 0), (0, width - v.shape[1])))


def _pad_cols(w, width=LANES):
    return jnp.pad(w, ((0, 0), (0, width - w.shape[1])))


def _head_expander(n_heads, head_width, rows=LANES):
    r = jnp.arange(rows)[:, None]
    col_head = jnp.arange(n_heads * head_width)[None, :] // head_width
    return (r == col_head).astype(BF16)


def _block_diag(blocks):
    g, r, c = blocks.shape
    eye = jnp.eye(g, dtype=blocks.dtype)
    return (eye[:, None, :, None] * blocks[:, :, None, :]).reshape(g * r, g * c)


def kernel(x, l0_norm_mix, l0_w_in, ssd_conv_w, ssd_conv_b, ssd_dt_bias, ssd_A_log, ssd_D, ssd_norm_w, l0_w_out, l0_norm_mlp, l0_w_up, l0_w_down, l1_norm_mix, l1_w_in, gdn_conv_w, gdn_A_log, gdn_dt_bias, gdn_norm_w, s5_A_re, s5_A_im, s5_log_step, s5_B_re, s5_B_im, s5_C_re, s5_C_im, s5_D, s5_w_glu, s5_b_glu, l1_w_out, l1_norm_mlp, l1_w_up, l1_w_down, final_norm):
    bsz, seqlen, d = x.shape
    x2d = x.reshape(bsz * seqlen, d)
    row = lambda v: v.reshape(1, -1).astype(F32)
    bf = lambda w: w.astype(BF16)

    ret_w = RET_HEADS * RET_DK
    ssd_inner = SSD_HEADS * SSD_P
    ssd_xbc = ssd_inner + 2 * SSD_GROUPS * SSD_N
    offs, acc = [], 0
    for wdt in (ret_w, ret_w, ret_w, ret_w, ssd_inner, ssd_inner, ssd_xbc - ssd_inner, SSD_HEADS):
        offs.append((acc, acc + wdt))
        acc += wdt
    wq, wk, wv, wgate, wz, wxs, wbc, wdt_ = [l0_w_in[:, a:b] for a, b in offs]
    half = RET_DK // 2
    inv = ROPE_THETA ** (-jnp.arange(half, dtype=F32) / half)
    inv2 = jnp.concatenate([inv, inv]).reshape(1, RET_DK)
    sgn = jnp.concatenate([-jnp.ones((half,), F32), jnp.ones((half,), F32)]).reshape(1, RET_DK)
    xs, bc, q, k, v, gate, z, dt = _norm_proj(
        x2d, row(l0_norm_mix), [bf(w) for w in (wxs, wbc, wq, wk, wv, wgate, wz, _pad_cols(wdt_))],
        ("conv_bias", "conv_bias", "rope_q", "rope_k", "plain", "plain", "plain", "plain"), seqlen,
        conv_w=ssd_conv_w.astype(F32), conv_b=row(ssd_conv_b), rope=(inv2, sgn),
        out_dtypes=(F32, BF16, BF16, BF16, BF16, BF16, BF16, F32))

    ret_out = _retention(q, k, v, gate, bsz, seqlen)
    ssd_out = _ssd(z, xs, bc, dt, _pad_lanes(ssd_dt_bias), _pad_lanes(ssd_A_log),
                   row(jnp.repeat(ssd_D.astype(F32), SSD_P)), row(ssd_norm_w),
                   _head_expander(SSD_HEADS, SSD_P), bsz, seqlen)

    x2d = _out_mlp(x2d, [ret_out, ssd_out], [bf(l0_w_out[:ret_w]), bf(l0_w_out[ret_w:])],
                   row(l0_norm_mlp), bf(l0_w_up), bf(l0_w_down))

    gdn_w = GDN_HEADS * GDN_DK
    offs, acc = [], 0
    for wdt in (gdn_w, gdn_w, gdn_w, gdn_w, GDN_HEADS, GDN_HEADS, S5_CH):
        offs.append((acc, acc + wdt))
        acc += wdt
    w1 = [l1_w_in[:, a:b] for a, b in offs]
    w1[4] = _pad_cols(w1[4])
    w1[5] = _pad_cols(w1[5])
    qg, kg, vg, zg, b_raw, a_raw, u = _norm_proj(
        x2d, row(l1_norm_mix), [bf(w) for w in w1],
        ("conv", "conv", "conv", "plain", "plain", "plain", "plain"), seqlen, conv_w=gdn_conv_w.astype(F32))

    gdn_out = _gdn(qg, kg, vg, zg, b_raw, a_raw, _pad_lanes(gdn_A_log), _pad_lanes(gdn_dt_bias), row(gdn_norm_w),
                   _head_expander(GDN_HEADS, GDN_DK), bsz, seqlen)

    flat = lambda p: p.reshape(1, S5_GROUPS * S5_STATE).astype(F32)
    b_re = _block_diag(jnp.swapaxes(s5_B_re.astype(F32), 1, 2))
    b_im = _block_diag(jnp.swapaxes(s5_B_im.astype(F32), 1, 2))
    c_re = _block_diag(jnp.swapaxes(s5_C_re.astype(F32), 1, 2))
    c_im = _block_diag(jnp.swapaxes(s5_C_im.astype(F32), 1, 2))
    s5_out = _s5(u, flat(s5_A_re), flat(s5_A_im), row(jnp.repeat(s5_log_step.astype(F32), S5_STATE)),
                 b_re, b_im, bf(c_re), bf(c_im), row(s5_D), bf(s5_w_glu), row(s5_b_glu), bsz, seqlen)

    out = _out_mlp(x2d, [gdn_out, s5_out], [bf(l1_w_out[:gdn_w]), bf(l1_w_out[gdn_w:])],
                   row(l1_norm_mlp), bf(l1_w_up), bf(l1_w_down), final_gain=row(final_norm))
    return out.reshape(bsz, seqlen, d)
```

```python
import functools
import math

import jax
import jax.numpy as jnp
from jax import lax
from jax.experimental import pallas as pl
from jax.experimental.pallas import tpu as pltpu

F32 = jnp.float32
BF16 = jnp.bfloat16
EPS = 1e-6
CONV_WIDTH = 4
HALO = 8

RET_HEADS, RET_DK = 4, 128
SSD_HEADS, SSD_P, SSD_GROUPS, SSD_N = 8, 64, 2, 128
GDN_HEADS, GDN_DK = 6, 128
GDN_CHUNK = 64
S5_CH, S5_GROUPS, S5_GROUP, S5_STATE = 256, 16, 16, 64
ROPE_THETA = 10000.0
LANES = 128

V7X_VMEM_BYTES = 64 * 1024 * 1024
VMEM_LIMIT = V7X_VMEM_BYTES - 8 * 1024 * 1024

NORM_PROJ_ROWS, NORM_PROJ_SUB_ROWS = 512, 128
MLP_ROWS, FF_CHUNK = 1024, 1024
MIXER_ROWS = 256


def _full(shape):
    nd = len(shape)
    return pl.BlockSpec(shape, lambda *_: (0,) * nd)


def _resident(shape):
    nd = len(shape)
    return pl.BlockSpec(shape, lambda *_: (0,) * nd, pipeline_mode=pl.Buffered(1))


def _params(n_axes):
    return pltpu.CompilerParams(dimension_semantics=("arbitrary",) * n_axes, vmem_limit_bytes=VMEM_LIMIT)


def _mm(a, b):
    return jnp.dot(a.astype(BF16), b.astype(BF16), preferred_element_type=F32)


def _mm_nt(a, b):
    return lax.dot_general(a.astype(BF16), b.astype(BF16), (((1,), (1,)), ((), ())), preferred_element_type=F32)


def _bmm(a, b):
    return lax.dot_general(a.astype(BF16), b.astype(BF16), (((2,), (1,)), ((0,), (0,))), preferred_element_type=F32)


def _bmm_nt(a, b):
    return lax.dot_general(a.astype(BF16), b.astype(BF16), (((2,), (2,)), ((0,), (0,))), preferred_element_type=F32)


def _split3(a):
    hi = a.astype(BF16)
    r1 = a - hi.astype(F32)
    mid = r1.astype(BF16)
    lo = (r1 - mid.astype(F32)).astype(BF16)
    return hi, mid, lo


def _rows01(m01, a):
    n = a.shape[1]
    p = jnp.dot(m01.astype(BF16), jnp.concatenate(_split3(a), axis=1), preferred_element_type=F32)
    return p[:, :n] + p[:, n:2 * n] + p[:, 2 * n:]


def _cols01(a, e01):
    hi, mid, lo = _split3(a)
    dot = lambda x: jnp.dot(x, e01, preferred_element_type=F32)
    return dot(hi) + dot(mid) + dot(lo)


def _sigmoid(x):
    return 1.0 / (1.0 + jnp.exp(-x))


def _silu(x):
    return x * _sigmoid(x)


def _softplus(x):
    return jnp.maximum(x, 0.0) + jnp.log(1.0 + jnp.exp(-jnp.abs(x)))


def _rms(x):
    return x * lax.rsqrt(jnp.mean(x * x, axis=-1, keepdims=True) + EPS)


def _norm_proj_body(*refs, kinds, tiles_per_seq, sub_rows):
    it = iter(refs)
    x_ref, g_ref = next(it), next(it)
    w_refs = [next(it) for _ in kinds]
    has_conv = any(k.startswith("conv") for k in kinds)
    has_rope = any(k.startswith("rope") for k in kinds)
    cw_ref = next(it) if has_conv else None
    cb_ref = next(it) if "conv_bias" in kinds else None
    inv_ref, sgn_ref = (next(it), next(it)) if has_rope else (None, None)
    o_refs = [next(it) for _ in kinds]
    xe_ref = next(it) if has_conv else None
    rc_ref, rs_ref = (next(it), next(it)) if has_rope else (None, None)

    tm = x_ref.shape[0]
    t_in_seq = pl.program_id(0) % tiles_per_seq

    if has_conv:
        @pl.when(t_in_seq == 0)
        def _():
            xe_ref[:, 0:HALO, :] = jnp.zeros((xe_ref.shape[0], HALO, LANES), F32)

    if has_rope:
        @pl.when(pl.program_id(0) == 0)
        def _():
            ang_r = lax.broadcasted_iota(jnp.int32, (tm, 1), 0).astype(F32) * inv_ref[...]
            rc_ref[...] = jnp.cos(ang_r)
            rs_ref[...] = jnp.sin(ang_r)

        ang_0 = (t_in_seq * tm).astype(F32) * inv_ref[...]
        c0, s0 = jnp.cos(ang_0), jnp.sin(ang_0)
        cos = rc_ref[...] * c0 - rs_ref[...] * s0
        sin = (rs_ref[...] * c0 + rc_ref[...] * s0) * sgn_ref[...]

    subs = list(range(0, tm, sub_rows))
    xbs = [(_rms(x_ref[r0:r0 + sub_rows, :]) * g_ref[...]).astype(BF16) for r0 in subs]
    conv_col = 0
    for kind, w_ref, o_ref in zip(kinds, w_refs, o_refs):
        width = w_ref.shape[1]
        ys = [jnp.dot(xb, w_ref[...], preferred_element_type=F32) for xb in xbs]
        for r0, y in zip(subs, ys):
            rs = slice(r0, r0 + sub_rows)
            if kind.startswith("rope"):
                scale = RET_DK ** -0.5 if kind == "rope_q" else 1.0
                for h in range(width // RET_DK):
                    sl = slice(h * RET_DK, (h + 1) * RET_DK)
                    yh = y[:, sl]
                    rot = yh * cos[rs] + pltpu.roll(yh, RET_DK // 2, 1) * sin[rs]
                    o_ref[rs, sl] = (rot * scale).astype(o_ref.dtype)
            elif kind.startswith("conv"):
                for lt in range(width // LANES):
                    ct = conv_col // LANES + lt
                    ls = slice(conv_col + lt * LANES, conv_col + (lt + 1) * LANES)
                    xe_ref[ct, HALO + r0:HALO + r0 + sub_rows, :] = y[:, lt * LANES:(lt + 1) * LANES]
                    acc = None
                    for j in range(CONV_WIDTH):
                        off = HALO - (CONV_WIDTH - 1) + j + r0
                        term = cw_ref[j:j + 1, ls] * xe_ref[ct, off:off + sub_rows, :]
                        acc = term if acc is None else acc + term
                    if kind == "conv_bias":
                        acc = acc + cb_ref[:, ls]
                    o_ref[rs, lt * LANES:(lt + 1) * LANES] = _silu(acc).astype(o_ref.dtype)
            else:
                o_ref[rs, :] = y.astype(o_ref.dtype)
        if kind.startswith("conv"):
            for lt in range(width // LANES):
                ct = conv_col // LANES + lt
                xe_ref[ct, 0:HALO, :] = xe_ref[ct, tm:tm + HALO, :]
            conv_col += width


def _norm_proj(x2d, gain, weights, kinds, seqlen, conv_w=None, conv_b=None, rope=None, out_dtypes=None,
               tm=NORM_PROJ_ROWS, sub_rows=NORM_PROJ_SUB_ROWS):
    t, d = x2d.shape
    out_dtypes = out_dtypes or (F32,) * len(weights)
    in_specs = [pl.BlockSpec((tm, d), lambda i: (i, 0)), _full((1, d))] + [_resident(w.shape) for w in weights]
    args = [x2d, gain, *weights]
    scratch = []
    if conv_w is not None:
        in_specs.append(_full(conv_w.shape))
        args.append(conv_w)
        if conv_b is not None:
            in_specs.append(_full(conv_b.shape))
            args.append(conv_b)
        scratch.append(pltpu.VMEM((conv_w.shape[1] // LANES, tm + HALO, LANES), F32))
    if rope is not None:
        in_specs += [_full(r.shape) for r in rope]
        args += list(rope)
        scratch += [pltpu.VMEM((tm, RET_DK), F32), pltpu.VMEM((tm, RET_DK), F32)]
    return pl.pallas_call(
        functools.partial(_norm_proj_body, kinds=tuple(kinds), tiles_per_seq=seqlen // tm, sub_rows=sub_rows),
        grid=(t // tm,),
        in_specs=in_specs,
        out_specs=[pl.BlockSpec((tm, w.shape[1]), lambda i: (i, 0)) for w in weights],
        out_shape=[jax.ShapeDtypeStruct((t, w.shape[1]), dt) for w, dt in zip(weights, out_dtypes)],
        scratch_shapes=scratch,
        compiler_params=_params(1),
        name="norm_proj",
    )(*args)


def _out_mlp_body(x_ref, *refs, n_mix, ff_chunk, final):
    mix_refs, wo_refs = refs[:n_mix], refs[n_mix:2 * n_mix]
    rest = refs[2 * n_mix:]
    if final:
        g_ref, wu_ref, wd_ref, gf_ref, o_ref = rest
    else:
        g_ref, wu_ref, wd_ref, o_ref = rest
    d_ff = wu_ref.shape[1]
    x1 = x_ref[...]
    for m_ref, w_ref in zip(mix_refs, wo_refs):
        x1 = x1 + jnp.dot(m_ref[...].astype(BF16), w_ref[...], preferred_element_type=F32)
    xb = (_rms(x1) * g_ref[...]).astype(BF16)
    mlp = None
    for c in range(d_ff // ff_chunk):
        a = jnp.dot(xb, wu_ref[:, c * ff_chunk:(c + 1) * ff_chunk], preferred_element_type=F32)
        a = jnp.maximum(a, 0.0)
        t = jnp.dot((a * a).astype(BF16), wd_ref[c * ff_chunk:(c + 1) * ff_chunk, :],
                    preferred_element_type=F32)
        mlp = t if mlp is None else mlp + t
    acc = x1 + mlp
    if final:
        acc = _rms(acc) * gf_ref[...]
    o_ref[...] = acc


def _out_mlp(x2d, mixes, w_outs, gain, w_up, w_down, final_gain=None, tm=MLP_ROWS, ff_chunk=FF_CHUNK):
    t, d = x2d.shape
    n_mix = len(mixes)
    final = final_gain is not None
    row = lambda w: pl.BlockSpec((tm, w), lambda i: (i, 0))
    in_specs = [row(d)] + [row(m.shape[1]) for m in mixes] + [_resident(w.shape) for w in w_outs]
    in_specs += [_full((1, d)), _resident(w_up.shape), _resident(w_down.shape)]
    args = [x2d, *mixes, *w_outs, gain, w_up, w_down]
    if final:
        in_specs.append(_full((1, d)))
        args.append(final_gain)
    return pl.pallas_call(
        functools.partial(_out_mlp_body, n_mix=n_mix, ff_chunk=ff_chunk, final=final),
        grid=(t // tm,),
        in_specs=in_specs,
        out_specs=row(d),
        out_shape=jax.ShapeDtypeStruct((t, d), F32),
        compiler_params=_params(1),
        name="out_mlp",
    )(*args)


def _retention_body(q_ref, k_ref, v_ref, g_ref, o_ref, s_ref, dm_ref, *, chunk):
    bsz = q_ref.shape[0]
    c = chunk
    i = pl.program_id(0)
    gammas = [math.log(1.0 - 2.0 ** (-5.0 - h)) for h in range(RET_HEADS)]

    @pl.when(i == 0)
    def _():
        s_ref[...] = jnp.zeros_like(s_ref)
        rowi = lax.broadcasted_iota(jnp.int32, (c, c), 0)
        coli = lax.broadcasted_iota(jnp.int32, (c, c), 1)
        diff = (rowi - coli).astype(F32)
        for h in range(RET_HEADS):
            dm_ref[h] = jnp.where(rowi >= coli, jnp.exp(gammas[h] * diff), 0.0)

    pbh = [(b, h) for b in range(bsz) for h in range(RET_HEADS)]
    head = lambda ref, b, h: ref[b, :, h * RET_DK:(h + 1) * RET_DK]
    idx = lax.broadcasted_iota(jnp.int32, (c, 1), 0).astype(F32)
    q8 = jnp.stack([head(q_ref, b, h) for b, h in pbh])
    k8 = jnp.stack([head(k_ref, b, h) for b, h in pbh])
    v8 = jnp.stack([head(v_ref, b, h) for b, h in pbh])
    scores = _bmm_nt(q8, k8) * jnp.stack([dm_ref[h] for _, h in pbh])
    q_dec = jnp.stack([jnp.exp(gammas[h] * (idx + 1.0)) for _, h in pbh])
    state = s_ref[...]
    y = _bmm(scores, v8) + _bmm(q8 * q_dec, state)
    kw_t = jnp.stack([(head(k_ref, b, h) * jnp.exp(gammas[h] * (c - 1.0 - idx))).T for b, h in pbh])
    chunk_decay = jnp.stack([jnp.full((1, RET_DK), math.exp(gammas[h] * c), F32) for _, h in pbh])
    s_ref[...] = state * chunk_decay + _bmm(kw_t, v8)
    for p, (b, h) in enumerate(pbh):
        gt = head(g_ref, b, h).astype(F32)
        o_ref[b, :, h * RET_DK:(h + 1) * RET_DK] = (_silu(gt) * _rms(y[p])).astype(o_ref.dtype)


def _retention(q, k, v, gate, bsz, seqlen, chunk=MIXER_ROWS):
    nt = seqlen // chunk
    w = RET_HEADS * RET_DK
    row = pl.BlockSpec((bsz, chunk, w), lambda i: (0, i, 0))
    seq = lambda a: a.reshape(bsz, seqlen, w)
    return pl.pallas_call(
        functools.partial(_retention_body, chunk=chunk),
        grid=(nt,),
        in_specs=[row, row, row, row],
        out_specs=row,
        out_shape=jax.ShapeDtypeStruct((bsz, seqlen, w), BF16),
        scratch_shapes=[pltpu.VMEM((bsz * RET_HEADS, RET_DK, RET_DK), F32),
                        pltpu.VMEM((RET_HEADS, chunk, chunk), F32)],
        compiler_params=_params(1),
        name="retention",
    )(seq(q), seq(k), seq(v), seq(gate)).reshape(bsz * seqlen, w)


def _ssd_body(z_ref, xs_ref, bc_ref, dt_ref, dtb_ref, al_ref, de_ref, nw_ref, e8_ref, o_ref, s_ref, *, chunk):
    bsz = xs_ref.shape[0]
    c = chunk
    i = pl.program_id(0)
    inner = SSD_HEADS * SSD_P
    gw = inner // SSD_GROUPS
    hpg = SSD_HEADS // SSD_GROUPS
    rows_of = lambda ref: jnp.concatenate([ref[b] for b in range(bsz)], axis=0)

    @pl.when(i == 0)
    def _():
        s_ref[...] = jnp.zeros_like(s_ref)

    xs = rows_of(xs_ref)
    dtv = _softplus(rows_of(dt_ref) + dtb_ref[...])
    la = dtv * (-jnp.exp(al_ref[...]))
    rowi = lax.broadcasted_iota(jnp.int32, (c, c), 0)
    coli = lax.broadcasted_iota(jnp.int32, (c, c), 1)
    causal = rowi >= coli
    cums = [_rows01(causal, la[b * c:(b + 1) * c]) for b in range(bsz)]
    cum_ts = [cm_.T for cm_ in cums]
    e8 = e8_ref[...]
    dt_e = _cols01(dtv, e8)
    cum_e = _cols01(jnp.concatenate(cums, axis=0), e8)
    last_e = jnp.concatenate([jnp.broadcast_to(cum_e[(b + 1) * c - 1:(b + 1) * c, :], (c, inner)) for b in range(bsz)],
                             axis=0)
    xd = xs * dt_e
    xdw = xd * jnp.exp(last_e - cum_e)
    lane = lax.broadcasted_iota(jnp.int32, (1, gw), 1)

    pb = [(b, g) for b in range(bsz) for g in range(SSD_GROUPS)]
    rs = lambda b: slice(b * c, (b + 1) * c)
    c_off = SSD_GROUPS * SSD_N
    bgs = jnp.stack([bc_ref[b, :, g * SSD_N:(g + 1) * SSD_N] for b, g in pb])
    cgs = jnp.stack([bc_ref[b, :, c_off + g * SSD_N:c_off + (g + 1) * SSD_N] for b, g in pb])
    cb = _bmm_nt(cgs, bgs)
    xd_p = jnp.stack([xd[rs(b), g * gw:(g + 1) * gw] for b, g in pb])
    yd = None
    for j in range(hpg):
        lmat = jnp.stack([jnp.exp(jnp.where(causal, cums[b][:, g * hpg + j:g * hpg + j + 1]
                                            - cum_ts[b][g * hpg + j:g * hpg + j + 1, :], -jnp.inf)) for b, g in pb])
        xm = jnp.where((lane >= SSD_P * j) & (lane < SSD_P * (j + 1)), xd_p, 0.0)
        t = _bmm(cb * lmat, xm)
        yd = t if yd is None else yd + t
    state = s_ref[...]
    cum_p = jnp.stack([cum_e[rs(b), g * gw:(g + 1) * gw] for b, g in pb])
    y = yd + _bmm(cgs, state) * jnp.exp(cum_p)
    dec = jnp.stack([jnp.exp(last_e[b * c:b * c + 1, g * gw:(g + 1) * gw]) for b, g in pb])
    bg_t = jnp.stack([bc_ref[b, :, g * SSD_N:(g + 1) * SSD_N].astype(F32).T for b, g in pb])
    s_ref[...] = state * dec + _bmm(bg_t, jnp.stack([xdw[rs(b), g * gw:(g + 1) * gw] for b, g in pb]))

    for p, (b, g) in enumerate(pb):
        gsl = slice(g * gw, (g + 1) * gw)
        yg = (y[p] + de_ref[:, gsl] * xs[rs(b), gsl]) * _silu(z_ref[b, :, gsl].astype(F32))
        o_ref[b, :, gsl] = (_rms(yg) * nw_ref[:, gsl]).astype(o_ref.dtype)


def _ssd(z, xs, bc, dt, dt_bias, a_log, d_e, norm_w, e8, bsz, seqlen, chunk=MIXER_ROWS):
    nt = seqlen // chunk
    inner = SSD_HEADS * SSD_P
    row = lambda w: pl.BlockSpec((bsz, chunk, w), lambda i: (0, i, 0))
    seq = lambda a: a.reshape(bsz, seqlen, a.shape[-1])
    return pl.pallas_call(
        functools.partial(_ssd_body, chunk=chunk),
        grid=(nt,),
        in_specs=[row(inner), row(inner), row(bc.shape[1]), row(LANES), _full((1, LANES)), _full((1, LANES)),
                  _full((1, inner)), _full((1, inner)), _full(e8.shape)],
        out_specs=row(inner),
        out_shape=jax.ShapeDtypeStruct((bsz, seqlen, inner), BF16),
        scratch_shapes=[pltpu.VMEM((bsz * SSD_GROUPS, SSD_N, inner // SSD_GROUPS), F32)],
        compiler_params=_params(1),
        name="ssd",
    )(seq(z), seq(xs), seq(bc), seq(dt), dt_bias, a_log, d_e, norm_w, e8).reshape(bsz * seqlen, inner)


def _gdn_body(q_ref, k_ref, v_ref, z_ref, b_ref, a_ref, al_ref, dtb_ref, nw_ref, e6_ref, o_ref, s_ref, *, tile):
    bsz = q_ref.shape[0]
    ck = GDN_CHUNK
    nck = tile // ck
    nall = bsz * nck
    i = pl.program_id(0)
    qk_w = GDN_HEADS * GDN_DK
    rows_of = lambda ref: jnp.concatenate([ref[b] for b in range(bsz)], axis=0)

    @pl.when(i == 0)
    def _():
        s_ref[...] = jnp.zeros_like(s_ref)

    beta = _sigmoid(rows_of(b_ref))
    gl = -jnp.exp(al_ref[...]) * _softplus(rows_of(a_ref) + dtb_ref[...])

    rowt = lax.broadcasted_iota(jnp.int32, (tile, tile), 0)
    colt = lax.broadcasted_iota(jnp.int32, (tile, tile), 1)
    chunk_causal = ((rowt >> 6) == (colt >> 6)) & (rowt >= colt)
    gc = jnp.concatenate([_rows01(chunk_causal, gl[b * tile:(b + 1) * tile]) for b in range(bsz)], axis=0)
    gc_t = gc.T
    e6 = e6_ref[...]
    gc_e = _cols01(gc, e6)
    beta_e = _cols01(beta, e6)
    eg_e = jnp.exp(gc_e)
    glast_rows = [jnp.broadcast_to(gc_e[(cix + 1) * ck - 1:(cix + 1) * ck, :], (ck, qk_w)) for cix in range(nall)]
    glast_e = jnp.concatenate(glast_rows, axis=0)
    tail_e = jnp.exp(glast_e - gc_e)
    dlast_e = jnp.exp(glast_e)

    rowi = lax.broadcasted_iota(jnp.int32, (ck, ck), 0)
    coli = lax.broadcasted_iota(jnp.int32, (ck, ck), 1)
    causal = rowi >= coli
    strict = rowi > coli

    def same_block(bits):
        return (rowi >> bits) == (coli >> bits)

    eye = (rowi == coli).astype(F32)
    base_mask = same_block(3) & strict
    level_masks = [same_block(b + 1) & (~same_block(b)) & strict for b in (3, 4, 5)]

    q_all, k_all, v_all = rows_of(q_ref), rows_of(k_ref), rows_of(v_ref)
    q_l, k_l, kb_l, rhs_l, kt_l, qg_l, decay_l = [], [], [], [], [], [], []
    for h in range(GDN_HEADS):
        sl = slice(h * GDN_DK, (h + 1) * GDN_DK)
        qh = q_all[:, sl]
        kh = k_all[:, sl]
        vh = v_all[:, sl]
        qh = qh * lax.rsqrt(jnp.sum(qh * qh, axis=-1, keepdims=True) + EPS) * (GDN_DK ** -0.5)
        kh = kh * lax.rsqrt(jnp.sum(kh * kh, axis=-1, keepdims=True) + EPS)
        bh = beta_e[:, sl]
        kb = kh * bh
        rhs = jnp.concatenate([vh * bh, kb * eg_e[:, sl]], axis=1)
        kt = kh * tail_e[:, sl]
        qg = qh * eg_e[:, sl]
        for cix in range(nall):
            rs = slice(cix * ck, (cix + 1) * ck)
            q_l.append(qh[rs])
            k_l.append(kh[rs])
            kb_l.append(kb[rs])
            rhs_l.append(rhs[rs])
            kt_l.append(kt[rs])
            qg_l.append(qg[rs])
            seg = gc[rs, h:h + 1] - gc_t[h:h + 1, rs]
            decay_l.append(jnp.exp(jnp.where(causal, seg, -jnp.inf)))
    decay = jnp.stack(decay_l)
    kq = _bmm_nt(jnp.stack([jnp.concatenate([kb, qq], axis=0) for kb, qq in zip(kb_l, q_l)]), jnp.stack(k_l))
    lower = jnp.where(strict, kq[:, :ck] * decay, 0.0)
    attn = kq[:, ck:] * decay

    d8 = jnp.where(base_mask, lower, 0.0)
    x2 = _bmm(d8, d8)
    x4 = _bmm(x2, x2)
    tinv = eye - d8
    tinv = tinv + _bmm(tinv, x2)
    tinv = tinv + _bmm(tinv, x4)
    for lm in level_masks:
        e = jnp.where(lm, lower, 0.0)
        tinv = tinv - _bmm(tinv, _bmm(e, tinv))

    uw = _bmm(tinv, jnp.stack(rhs_l))

    def of_chunk(xs, cix):
        return jnp.stack([xs[h * nall + b * nck + cix] for b in range(bsz) for h in range(GDN_HEADS)])

    state = s_ref[...]
    outs = []
    for cix in range(nck):
        uw_c = of_chunk(uw, cix)
        r = _bmm(jnp.concatenate([uw_c[:, :, GDN_DK:], of_chunk(qg_l, cix)], axis=1), state)
        v_new = uw_c[:, :, :GDN_DK] - r[:, :ck]
        kt_t = jnp.stack([kt_l[h * nall + b * nck + cix].T for b in range(bsz) for h in range(GDN_HEADS)])
        av = _bmm(jnp.concatenate([of_chunk(attn, cix), kt_t], axis=1), v_new)
        outs.append(r[:, ck:] + av[:, :ck])
        dl = jnp.stack([dlast_e[(b * nck + cix) * ck:(b * nck + cix) * ck + 1, h * GDN_DK:(h + 1) * GDN_DK]
                        for b in range(bsz) for h in range(GDN_HEADS)])
        state = state * dl + av[:, ck:]
    s_ref[...] = state
    for b in range(bsz):
        for h in range(GDN_HEADS):
            sl = slice(h * GDN_DK, (h + 1) * GDN_DK)
            o = jnp.concatenate([outs[cix][b * GDN_HEADS + h] for cix in range(nck)], axis=0)
            o_ref[b, :, sl] = (_rms(o) * nw_ref[...] * _silu(z_ref[b, :, sl])).astype(o_ref.dtype)


def _gdn(q, k, v, z, b_raw, a_raw, a_log, dt_bias, norm_w, e6, bsz, seqlen, tile=MIXER_ROWS):
    nt = seqlen // tile
    wv = GDN_HEADS * GDN_DK
    row = lambda w: pl.BlockSpec((bsz, tile, w), lambda i: (0, i, 0))
    seq = lambda a: a.reshape(bsz, seqlen, a.shape[-1])
    return pl.pallas_call(
        functools.partial(_gdn_body, tile=tile),
        grid=(nt,),
        in_specs=[row(wv), row(wv), row(wv), row(wv), row(LANES), row(LANES), _full((1, LANES)), _full((1, LANES)),
                  _full((1, GDN_DK)), _full(e6.shape)],
        out_specs=row(wv),
        out_shape=jax.ShapeDtypeStruct((bsz, seqlen, wv), BF16),
        scratch_shapes=[pltpu.VMEM((bsz * GDN_HEADS, GDN_DK, GDN_DK), F32)],
        compiler_params=_params(1),
        name="gdn",
    )(seq(q), seq(k), seq(v), seq(z), seq(b_raw), seq(a_raw), a_log, dt_bias, norm_w, e6).reshape(bsz * seqlen, wv)


def _cmul_add(ar, ai, br, bi, cr, ci):
    return ar * br - ai * bi + cr, ar * bi + ai * br + ci


def _s5_body(u_ref, are_ref, aim_ref, ls_ref, bre_ref, bim_ref, cre_ref, cim_ref, d_ref, wg_ref, bg_ref,
             o_ref, pre_ref, pim_ref, qre_ref, qim_ref, bsr_ref, bsi_ref, h_ref, hre_ref, him_ref, *, tile):
    bsz = u_ref.shape[0]
    tt = tile
    sub = 8
    seg = tt // sub
    seg_bits = seg.bit_length() - 1
    ns = S5_GROUPS * S5_STATE
    i = pl.program_id(0)

    @pl.when(i == 0)
    def _():
        step = jnp.exp(ls_ref[...])

        def powers(n, scale):
            tpos = (lax.broadcasted_iota(jnp.int32, (n, 1), 0).astype(F32) + 1.0) * scale
            mag = jnp.exp(tpos * (are_ref[...] * step))
            ang = tpos * (aim_ref[...] * step)
            return mag * jnp.cos(ang), mag * jnp.sin(ang)

        pre_ref[...], pim_ref[...] = powers(seg, 1.0)
        qre_ref[...], qim_ref[...] = powers(sub, float(seg))
        h_ref[...] = jnp.zeros_like(h_ref)
        a_re, a_im = are_ref[...], aim_ref[...]
        n_re, n_im = pre_ref[0:1, :] - 1.0, pim_ref[0:1, :]
        den = a_re * a_re + a_im * a_im
        c_re = (n_re * a_re + n_im * a_im) / den
        c_im = (n_im * a_re - n_re * a_im) / den
        b_re, b_im = bre_ref[...], bim_ref[...]
        bsr_ref[...] = (c_re * b_re - c_im * b_im).astype(BF16)
        bsi_ref[...] = (c_re * b_im + c_im * b_re).astype(BF16)

    row_i = lax.broadcasted_iota(jnp.int32, (tt, tt), 0)
    col_i = lax.broadcasted_iota(jnp.int32, (tt, tt), 1)
    perm = col_i == (row_i & (sub - 1)) * seg + (row_i >> 3)
    unperm = col_i == (row_i & (seg - 1)) * sub + (row_i >> seg_bits)
    u = jnp.concatenate([_rows01(perm, u_ref[b]) for b in range(bsz)], axis=0)

    hre_ref[...] = _mm(u, bsr_ref[...])
    him_ref[...] = _mm(u, bsi_ref[...])

    sl_i = lax.broadcasted_iota(jnp.int32, (sub, LANES), 0)

    def down(x, d, fill):
        return jnp.where(sl_i >= d, pltpu.roll(x, d, 0), fill)

    for b in range(bsz):
        r0 = b * tt
        for j in range(ns // LANES):
            ls = slice(j * LANES, (j + 1) * LANES)
            ar, ai = pre_ref[0:1, ls], pim_ref[0:1, ls]
            gr, gi = hre_ref[r0:r0 + sub, ls], him_ref[r0:r0 + sub, ls]
            for t in range(1, seg):
                rs = slice(r0 + t * sub, r0 + (t + 1) * sub)
                gr, gi = _cmul_add(ar, ai, gr, gi, hre_ref[rs, ls], him_ref[rs, ls])
                hre_ref[rs, ls] = gr
                him_ref[rs, ls] = gi
            fr, fi = gr, gi
            for k in range(3):
                d = 1 << k
                mr, mi = qre_ref[d - 1:d, ls], qim_ref[d - 1:d, ls]
                fr, fi = _cmul_add(mr, mi, down(fr, d, 0.0), down(fi, d, 0.0), fr, fi)
            cr, ci = h_ref[2 * b:2 * b + 1, ls], h_ref[2 * b + 1:2 * b + 2, ls]
            zr, zi = _cmul_add(qre_ref[:, ls], qim_ref[:, ls], cr, ci, fr, fi)
            h_ref[2 * b:2 * b + 1, ls] = zr[sub - 1:sub, :]
            h_ref[2 * b + 1:2 * b + 2, ls] = zi[sub - 1:sub, :]
            er, ei = down(zr, 1, cr), down(zi, 1, ci)
            for t in range(seg):
                rs = slice(r0 + t * sub, r0 + (t + 1) * sub)
                hr, hi = _cmul_add(pre_ref[t:t + 1, ls], pim_ref[t:t + 1, ls], er, ei,
                                   hre_ref[rs, ls], him_ref[rs, ls])
                hre_ref[rs, ls] = hr
                him_ref[rs, ls] = hi

    y = _mm(hre_ref[...], cre_ref[...]) - _mm(him_ref[...], cim_ref[...]) + d_ref[...] * u
    y = jax.nn.gelu(y)
    out = y * _sigmoid(_mm(y, wg_ref[...]) + bg_ref[...])
    for b in range(bsz):
        o_ref[b] = _mm(unperm, out[b * tt:(b + 1) * tt]).astype(o_ref.dtype)


def _s5(u, a_re, a_im, log_step, b_re, b_im, c_re, c_im, d_skip, w_glu, b_glu, bsz, seqlen, tile=MIXER_ROWS):
    nt = seqlen // tile
    ns = S5_GROUPS * S5_STATE
    row = pl.BlockSpec((bsz, tile, S5_CH), lambda i: (0, i, 0))
    vec = _full((1, ns))
    carry_rows = HALO * pl.cdiv(2 * bsz, HALO)
    return pl.pallas_call(
        functools.partial(_s5_body, tile=tile),
        grid=(nt,),
        in_specs=[row, vec, vec, vec, _full((S5_CH, ns)), _full((S5_CH, ns)), _full((ns, S5_CH)),
                  _full((ns, S5_CH)), _full((1, S5_CH)), _full((S5_CH, S5_CH)), _full((1, S5_CH))],
        out_specs=row,
        out_shape=jax.ShapeDtypeStruct((bsz, seqlen, S5_CH), BF16),
        scratch_shapes=[pltpu.VMEM((tile // 8, ns), F32), pltpu.VMEM((tile // 8, ns), F32),
                        pltpu.VMEM((8, ns), F32), pltpu.VMEM((8, ns), F32),
                        pltpu.VMEM((S5_CH, ns), BF16), pltpu.VMEM((S5_CH, ns), BF16),
                        pltpu.VMEM((carry_rows, ns), F32),
                        pltpu.VMEM((bsz * tile, ns), F32), pltpu.VMEM((bsz * tile, ns), F32)],
        compiler_params=_params(1),
        name="s5",
    )(u.reshape(bsz, seqlen, S5_CH), a_re, a_im, log_step, b_re, b_im, c_re, c_im, d_skip, w_glu, b_glu
      ).reshape(bsz * seqlen, S5_CH)


def _pad_lanes(v, width=LANES):
    v = v.reshape(1, -1).astype(F32)
    return jnp.pad(v, ((0, 0), (0, width - v.shape[1])))


def _pad_cols(w, width=LANES):
    return jnp.pad(w, ((0, 0), (0, width - w.shape[1])))


def _head_expander(n_heads, head_width, rows=LANES):
    r = jnp.arange(rows)[:, None]
    col_head = jnp.arange(n_heads * head_width)[None, :] // head_width
    return (r == col_head).astype(BF16)


def _block_diag(blocks):
    g, r, c = blocks.shape
    eye = jnp.eye(g, dtype=blocks.dtype)
    return (eye[:, None, :, None] * blocks[:, :, None, :]).reshape(g * r, g * c)


def kernel(x, l0_norm_mix, l0_w_in, ssd_conv_w, ssd_conv_b, ssd_dt_bias, ssd_A_log, ssd_D, ssd_norm_w, l0_w_out, l0_norm_mlp, l0_w_up, l0_w_down, l1_norm_mix, l1_w_in, gdn_conv_w, gdn_A_log, gdn_dt_bias, gdn_norm_w, s5_A_re, s5_A_im, s5_log_step, s5_B_re, s5_B_im, s5_C_re, s5_C_im, s5_D, s5_w_glu, s5_b_glu, l1_w_out, l1_norm_mlp, l1_w_up, l1_w_down, final_norm):
    bsz, seqlen, d = x.shape
    assert seqlen % NORM_PROJ_ROWS == 0 and seqlen % MIXER_ROWS == 0 and (bsz * seqlen) % MLP_ROWS == 0, x.shape
    x2d = x.reshape(bsz * seqlen, d)
    row = lambda v: v.reshape(1, -1).astype(F32)
    bf = lambda w: w.astype(BF16)

    ret_w = RET_HEADS * RET_DK
    ssd_inner = SSD_HEADS * SSD_P
    ssd_xbc = ssd_inner + 2 * SSD_GROUPS * SSD_N
    offs, acc = [], 0
    for wdt in (ret_w, ret_w, ret_w, ret_w, ssd_inner, ssd_inner, ssd_xbc - ssd_inner, SSD_HEADS):
        offs.append((acc, acc + wdt))
        acc += wdt
    wq, wk, wv, wgate, wz, wxs, wbc, wdt_ = [l0_w_in[:, a:b] for a, b in offs]
    half = RET_DK // 2
    inv = ROPE_THETA ** (-jnp.arange(half, dtype=F32) / half)
    inv2 = jnp.concatenate([inv, inv]).reshape(1, RET_DK)
    sgn = jnp.concatenate([-jnp.ones((half,), F32), jnp.ones((half,), F32)]).reshape(1, RET_DK)
    xs, bc, q, k, v, gate, z, dt = _norm_proj(
        x2d, row(l0_norm_mix), [bf(w) for w in (wxs, wbc, wq, wk, wv, wgate, wz, _pad_cols(wdt_))],
        ("conv_bias", "conv_bias", "rope_q", "rope_k", "plain", "plain", "plain", "plain"), seqlen,
        conv_w=ssd_conv_w.astype(F32), conv_b=row(ssd_conv_b), rope=(inv2, sgn),
        out_dtypes=(F32, BF16, BF16, BF16, BF16, BF16, BF16, F32))

    ret_out = _retention(q, k, v, gate, bsz, seqlen)
    ssd_out = _ssd(z, xs, bc, dt, _pad_lanes(ssd_dt_bias), _pad_lanes(ssd_A_log),
                   row(jnp.repeat(ssd_D.astype(F32), SSD_P)), row(ssd_norm_w),
                   _head_expander(SSD_HEADS, SSD_P), bsz, seqlen)

    x2d = _out_mlp(x2d, [ret_out, ssd_out], [bf(l0_w_out[:ret_w]), bf(l0_w_out[ret_w:])],
                   row(l0_norm_mlp), bf(l0_w_up), bf(l0_w_down))

    gdn_w = GDN_HEADS * GDN_DK
    offs, acc = [], 0
    for wdt in (gdn_w, gdn_w, gdn_w, gdn_w, GDN_HEADS, GDN_HEADS, S5_CH):
        offs.append((acc, acc + wdt))
        acc += wdt
    w1 = [l1_w_in[:, a:b] for a, b in offs]
    w1[4] = _pad_cols(w1[4])
    w1[5] = _pad_cols(w1[5])
    qg, kg, vg, zg, b_raw, a_raw, u = _norm_proj(
        x2d, row(l1_norm_mix), [bf(w) for w in w1],
        ("conv", "conv", "conv", "plain", "plain", "plain", "plain"), seqlen, conv_w=gdn_conv_w.astype(F32))

    gdn_out = _gdn(qg, kg, vg, zg, b_raw, a_raw, _pad_lanes(gdn_A_log), _pad_lanes(gdn_dt_bias), row(gdn_norm_w),
                   _head_expander(GDN_HEADS, GDN_DK), bsz, seqlen)

    flat = lambda p: p.reshape(1, S5_GROUPS * S5_STATE).astype(F32)
    b_re = _block_diag(jnp.swapaxes(s5_B_re.astype(F32), 1, 2))
    b_im = _block_diag(jnp.swapaxes(s5_B_im.astype(F32), 1, 2))
    c_re = _block_diag(jnp.swapaxes(s5_C_re.astype(F32), 1, 2))
    c_im = _block_diag(jnp.swapaxes(s5_C_im.astype(F32), 1, 2))
    s5_out = _s5(u, flat(s5_A_re), flat(s5_A_im), row(jnp.repeat(s5_log_step.astype(F32), S5_STATE)),
                 b_re, b_im, bf(c_re), bf(c_im), row(s5_D), bf(s5_w_glu), row(s5_b_glu), bsz, seqlen)

    out = _out_mlp(x2d, [gdn_out, s5_out], [bf(l1_w_out[:gdn_w]), bf(l1_w_out[gdn_w:])],
                   row(l1_norm_mlp), bf(l1_w_up), bf(l1_w_down), final_gain=row(final_norm))
    return out.reshape(bsz, seqlen, d)
```
